```python
import math
import jax, jax.numpy as jnp
from jax import lax
import numpy as np

D_MODEL = 2048
BATCH = 4
SEQ = 2048
DEPTH = 4

PE_DIM = 256
GRID_W = 64
N_MIXERS = 3
HEAD_DIM = 128
MIX_WIDTH = D_MODEL
EPS = 1e-6
NEG_INF = -1e30

A_HEADS = MIX_WIDTH // HEAD_DIM
A_KV_HEADS = max(A_HEADS // 4, 1)
A_WINDOW = 128
A_BLOCK = 128
A_Q = A_HEADS * HEAD_DIM
A_KV = A_KV_HEADS * HEAD_DIM
A_IN = A_Q + 2 * A_KV + MIX_WIDTH

B_HEADS = MIX_WIDTH // HEAD_DIM
NB_WIN_H = 8
NB_WIN_W = 16
B_IN = 3 * B_HEADS * HEAD_DIM + MIX_WIDTH

C_HEADS = MIX_WIDTH // (2 * HEAD_DIM)
C_QK_DIM = HEAD_DIM
C_V_DIM = 2 * HEAD_DIM
C_BLOCK = 128
C_QK = C_HEADS * 2 * C_QK_DIM
C_V = C_HEADS * C_V_DIM
C_IN = 2 * C_QK + C_V + MIX_WIDTH

N_A = (DEPTH + 2) // 3
N_B = (DEPTH + 1) // 3
N_C = DEPTH // 3

kernel_name = "hybrid_interleaved_bidir_encoder"


def rmsnorm(x, g):
    xf = x.astype(jnp.float32)
    y = xf * lax.rsqrt(jnp.mean(xf * xf, axis=-1, keepdims=True) + EPS)
    return (y * g.astype(jnp.float32)).astype(x.dtype)


def alibi_slopes(n_heads):
    return jnp.asarray(np.array([2.0 ** (-8.0 * (h + 1) / n_heads) for h in range(n_heads)], dtype=np.float32))


def window_gqa(q, k, v, sink):
    B, S, Hq, Dh = q.shape
    Hkv = k.shape[2]
    G = Hq // Hkv
    nb = S // A_BLOCK
    pad = ((0, 0), (A_BLOCK, A_BLOCK), (0, 0), (0, 0))
    kb = jnp.pad(k, pad).reshape(B, nb + 2, A_BLOCK, Hkv, Dh)
    vb = jnp.pad(v, pad).reshape(B, nb + 2, A_BLOCK, Hkv, Dh)
    kw = jnp.concatenate([kb[:, :-2], kb[:, 1:-1], kb[:, 2:]], axis=2)
    vw = jnp.concatenate([vb[:, :-2], vb[:, 1:-1], vb[:, 2:]], axis=2)
    qb = q.reshape(B, nb, A_BLOCK, Hkv, G, Dh)
    s = jnp.einsum('bnqhgd,bnkhd->bnhgqk', qb, kw).astype(jnp.float32) * (Dh ** -0.5)
    qi = jnp.arange(A_BLOCK)[:, None]
    kr = jnp.arange(3 * A_BLOCK)[None, :]
    rel = A_BLOCK + qi - kr
    blk = jnp.arange(nb)[:, None, None]
    kpos = (blk - 1) * A_BLOCK + kr[None]
    valid = (jnp.abs(rel) <= A_WINDOW)[None] & (kpos >= 0) & (kpos < S)
    bias = -alibi_slopes(Hq)[:, None, None] * jnp.abs(rel).astype(jnp.float32)[None]
    s = s + bias.reshape(Hkv, G, A_BLOCK, 3 * A_BLOCK)[None, None]
    s = jnp.where(valid[None, :, None, None], s, NEG_INF)
    sink_l = jnp.broadcast_to(sink.astype(jnp.float32).reshape(Hkv, G)[None, None, :, :, None, None],
                              s.shape[:-1] + (1,))
    pr = jax.nn.softmax(jnp.concatenate([s, sink_l], axis=-1), axis=-1)[..., :-1]
    o = jnp.einsum('bnhgqk,bnkhd->bnqhgd', pr.astype(v.dtype), vw)
    return o.reshape(B, S, Hq * Dh)


def neighborhood_attn(q, k, v, rpb):
    B, S, H, Dh = q.shape
    rows = S // GRID_W
    kh = min(NB_WIN_H, rows)
    kw = NB_WIN_W
    qg = q.reshape(B, rows, GRID_W, H, Dh)
    kg = k.reshape(B, rows, GRID_W, H, Dh)
    vg = v.reshape(B, rows, GRID_W, H, Dh)
    col = jnp.arange(GRID_W)
    col_start = jnp.clip(col - kw // 2, 0, GRID_W - kw)
    col_valid = (col[None, :] >= col_start[:, None]) & (col[None, :] < col_start[:, None] + kw)
    dx_idx = jnp.clip(col[None, :] - col[:, None] + NB_WIN_W - 1, 0, 2 * NB_WIN_W - 2)
    scale = Dh ** -0.5
    rpb_f = rpb.astype(jnp.float32)

    def row_step(r):
        rs = jnp.clip(r - kh // 2, 0, rows - kh)
        kr = lax.dynamic_slice_in_dim(kg, rs, kh, axis=1)
        vr = lax.dynamic_slice_in_dim(vg, rs, kh, axis=1)
        qr = lax.dynamic_index_in_dim(qg, r, axis=1, keepdims=False)
        s = jnp.einsum('bqhd,bywhd->bhqyw', qr, kr).astype(jnp.float32) * scale
        dy_idx = rs + jnp.arange(kh) - r + NB_WIN_H - 1
        bias = rpb_f[:, dy_idx[None, :, None], dx_idx[:, None, :]]
        s = jnp.where(col_valid[:, None, :], s + bias[None], NEG_INF)
        pr = jax.nn.softmax(s.reshape(B, H, GRID_W, kh * GRID_W), axis=-1).reshape(s.shape)
        return jnp.einsum('bhqyw,bywhd->bqhd', pr.astype(v.dtype), vr)

    o = lax.map(row_step, jnp.arange(rows))
    return jnp.transpose(o, (1, 0, 2, 3, 4)).reshape(B, S, H * Dh)


def diff_attn(q, k, v, lam, lambda_init, subln):
    B, S, H, _, Dqk = q.shape
    nb = S // C_BLOCK
    qb = jnp.transpose(q.reshape(B, nb, C_BLOCK, H, 2, Dqk), (1, 0, 2, 3, 4, 5))
    slopes = alibi_slopes(H)
    kpos = jnp.arange(S)
    scale = Dqk ** -0.5

    def block_step(args):
        n, qblk = args
        s = jnp.einsum('bqhmd,bkhmd->bhmqk', qblk, k).astype(jnp.float32) * scale
        qpos = n * C_BLOCK + jnp.arange(C_BLOCK)
        dist = jnp.abs(qpos[:, None] - kpos[None, :]).astype(jnp.float32)
        s = s - slopes[:, None, None, None] * dist
        pr = jax.nn.softmax(s, axis=-1)
        pdiff = pr[:, :, 0] - lam * pr[:, :, 1]
        return jnp.einsum('bhqk,bkhd->bqhd', pdiff.astype(v.dtype), v)

    o = lax.map(block_step, (jnp.arange(nb), qb))
    o = jnp.transpose(o, (1, 0, 2, 3, 4)).reshape(B, S, H, v.shape[-1])
    o = rmsnorm(o, subln) * (1.0 - lambda_init)
    return o.reshape(B, S, H * v.shape[-1])


def setup_inputs(seed: int = 0) -> dict:
    key = jax.random.key(seed)
    ks = jax.random.split(key, 16)
    f32 = jnp.float32
    x = jax.random.normal(ks[0], (BATCH, SEQ, D_MODEL), f32)
    p = jax.random.normal(ks[1], (DEPTH, BATCH, SEQ, PE_DIM), f32)
    norm_pre = 1.0 + 0.05 * jax.random.normal(ks[2], (DEPTH, D_MODEL), f32)
    norm_post = 1.0 + 0.05 * jax.random.normal(ks[3], (DEPTH, D_MODEL), f32)
    w_out = jax.random.normal(ks[4], (DEPTH, MIX_WIDTH, D_MODEL), f32) * MIX_WIDTH ** -0.5
    pe_proj = jax.random.normal(ks[5], (DEPTH, PE_DIM, D_MODEL), f32) * PE_DIM ** -0.5
    pe_gate = jax.random.normal(ks[6], (DEPTH, D_MODEL, D_MODEL), f32) * D_MODEL ** -0.5
    a_w_in = jax.random.normal(ks[7], (N_A, D_MODEL, A_IN), f32) * D_MODEL ** -0.5
    a_sink = 0.5 * jax.random.normal(ks[8], (N_A, A_HEADS), f32)
    b_w_in = jax.random.normal(ks[9], (N_B, D_MODEL, B_IN), f32) * D_MODEL ** -0.5
    b_rpb = 0.1 * jax.random.normal(ks[10], (N_B, B_HEADS, 2 * NB_WIN_H - 1, 2 * NB_WIN_W - 1), f32)
    c_w_in = jax.random.normal(ks[11], (N_C, D_MODEL, C_IN), f32) * D_MODEL ** -0.5
    c_lambda = 0.1 * jax.random.normal(ks[12], (N_C, 4, C_QK_DIM), f32)
    c_subln = 1.0 + 0.05 * jax.random.normal(ks[13], (N_C, C_V_DIM), f32)
    return {"x": x, "p": p, "norm_pre": norm_pre, "norm_post": norm_post, "w_out": w_out,
            "pe_proj": pe_proj, "pe_gate": pe_gate, "a_w_in": a_w_in, "a_sink": a_sink,
            "b_w_in": b_w_in, "b_rpb": b_rpb, "c_w_in": c_w_in, "c_lambda": c_lambda,
            "c_subln": c_subln}


def reference(x, p, norm_pre, norm_post, w_out, pe_proj, pe_gate, a_w_in, a_sink,
              b_w_in, b_rpb, c_w_in, c_lambda, c_subln):
    B, S, _ = x.shape
    for i in range(DEPTH):
        kind = i % N_MIXERS
        j = i // N_MIXERS
        h = rmsnorm(x, norm_pre[i])
        if kind == 0:
            proj = h @ a_w_in[j]
            q, k, v, g = jnp.split(proj, [A_Q, A_Q + A_KV, A_Q + 2 * A_KV], axis=-1)
            o = window_gqa(q.reshape(B, S, A_HEADS, HEAD_DIM), k.reshape(B, S, A_KV_HEADS, HEAD_DIM),
                           v.reshape(B, S, A_KV_HEADS, HEAD_DIM), a_sink[j])
        elif kind == 1:
            proj = h @ b_w_in[j]
            w = B_HEADS * HEAD_DIM
            q, k, v, g = jnp.split(proj, [w, 2 * w, 3 * w], axis=-1)
            o = neighborhood_attn(q.reshape(B, S, B_HEADS, HEAD_DIM), k.reshape(B, S, B_HEADS, HEAD_DIM),
                                  v.reshape(B, S, B_HEADS, HEAD_DIM), b_rpb[j])
        else:
            proj = h @ c_w_in[j]
            q, k, v, g = jnp.split(proj, [C_QK, 2 * C_QK, 2 * C_QK + C_V], axis=-1)
            lam_p = c_lambda[j].astype(jnp.float32)
            lambda_init = 0.8 - 0.6 * math.exp(-0.3 * i)
            lam = jnp.exp(jnp.sum(lam_p[0] * lam_p[1])) - jnp.exp(jnp.sum(lam_p[2] * lam_p[3])) + lambda_init
            o = diff_attn(q.reshape(B, S, C_HEADS, 2, C_QK_DIM), k.reshape(B, S, C_HEADS, 2, C_QK_DIM),
                          v.reshape(B, S, C_HEADS, C_V_DIM), lam, lambda_init, c_subln[j])
        y = (o * jax.nn.silu(g)) @ w_out[i]
        x = x + rmsnorm(y, norm_post[i])
        gate = jax.nn.sigmoid((x @ pe_gate[i]).astype(jnp.float32)).astype(x.dtype)
        x = x + gate * (p[i] @ pe_proj[i])
    return x
```

```python
import functools
import math

import jax
import jax.numpy as jnp
import numpy as np
from jax import lax
from jax.experimental import pallas as pl
from jax.experimental.pallas import tpu as pltpu

F32 = jnp.float32
BF16 = jnp.bfloat16

D_MODEL = 2048
BATCH = 4
SEQ = 2048
DEPTH = 4
TOKENS = BATCH * SEQ
PE_DIM = 256
GRID_W = 64
GRID_ROWS = SEQ // GRID_W
N_MIXERS = 3
HEAD_DIM = 128
MIX_WIDTH = D_MODEL
EPS = 1e-6
NEG_INF = -1e30
SCALE = HEAD_DIM ** -0.5

A_HEADS = 16
A_KV_HEADS = 4
A_GROUP = A_HEADS // A_KV_HEADS
A_WINDOW = 128
A_Q = A_HEADS * HEAD_DIM
A_KV = A_KV_HEADS * HEAD_DIM
A_IN = A_Q + 2 * A_KV + MIX_WIDTH

B_HEADS = 16
NB_WIN_H = 8
NB_WIN_W = 16
NB_DY = 2 * NB_WIN_H - 1
NB_DX = 2 * NB_WIN_W - 1
B_IN = 3 * B_HEADS * HEAD_DIM + MIX_WIDTH

C_HEADS = 8
C_V_DIM = 2 * HEAD_DIM
C_IN = 4 * MIX_WIDTH

VMEM_LIMIT_BYTES = 52 * 1024 * 1024


def _alibi_slopes(n_heads):
    return [2.0 ** (-8.0 * (h + 1) / n_heads) for h in range(n_heads)]


def _params(*sem):
    return pltpu.CompilerParams(dimension_semantics=sem, vmem_limit_bytes=VMEM_LIMIT_BYTES)


def _rms(xf, w):
    ms = jnp.mean(xf * xf, axis=-1, keepdims=True)
    return xf * lax.rsqrt(ms + EPS) * w


def _silu(g):
    return g * jax.nn.sigmoid(g)


def _prenorm_kernel(x_ref, w_ref, h_ref):
    h_ref[...] = _rms(x_ref[...], w_ref[...]).astype(BF16)


def _prenorm(x2d, w):
    tm = 512
    return pl.pallas_call(
        _prenorm_kernel,
        grid=(TOKENS // tm,),
        in_specs=[pl.BlockSpec((tm, D_MODEL), lambda i: (i, 0)),
                  pl.BlockSpec((1, D_MODEL), lambda i: (0, 0))],
        out_specs=pl.BlockSpec((tm, D_MODEL), lambda i: (i, 0)),
        out_shape=jax.ShapeDtypeStruct((TOKENS, D_MODEL), BF16),
        compiler_params=_params("arbitrary"),
        name="prenorm",
    )(x2d, w)


def _mm_kernel(h_ref, w_ref, o_ref):
    o_ref[...] = jnp.dot(h_ref[...], w_ref[...], preferred_element_type=F32).astype(o_ref.dtype)


def _in_proj(h, w):
    n = w.shape[1]
    tm, tn = 1024, 1024
    return pl.pallas_call(
        _mm_kernel,
        grid=(TOKENS // tm, n // tn),
        in_specs=[pl.BlockSpec((tm, D_MODEL), lambda i, j: (i, 0)),
                  pl.BlockSpec((D_MODEL, tn), lambda i, j: (0, j))],
        out_specs=pl.BlockSpec((tm, tn), lambda i, j: (i, j)),
        out_shape=jax.ShapeDtypeStruct((TOKENS, n), BF16),
        compiler_params=_params("arbitrary", "arbitrary"),
        name="in_proj",
    )(h, w)


A_TQ = 256


def _attn_a_kernel(sink_ref, q_ref, kp_ref, km_ref, kn_ref, vp_ref, vm_ref, vn_ref, g_ref, o_ref):
    n = pl.program_id(1)
    kw = A_TQ + 2 * A_WINDOW
    qi = lax.broadcasted_iota(jnp.int32, (A_TQ, kw), 0)
    kr = lax.broadcasted_iota(jnp.int32, (A_TQ, kw), 1)
    rel = qi + A_WINDOW - kr
    spos = n * A_TQ - A_WINDOW + kr
    valid = (jnp.abs(rel) <= A_WINDOW) & (spos >= 0) & (spos < SEQ)
    dist = jnp.abs(rel).astype(F32)
    slopes = _alibi_slopes(A_HEADS)
    for kv in range(A_KV_HEADS):
        c = slice(kv * HEAD_DIM, (kv + 1) * HEAD_DIM)
        k = jnp.concatenate([kp_ref[:, c], km_ref[:, c], kn_ref[:, c]], axis=0)
        v = jnp.concatenate([vp_ref[:, c], vm_ref[:, c], vn_ref[:, c]], axis=0)
        for g in range(A_GROUP):
            h = kv * A_GROUP + g
            hc = slice(h * HEAD_DIM, (h + 1) * HEAD_DIM)
            s = lax.dot_general(q_ref[:, hc], k, (((1,), (1,)), ((), ())),
                                preferred_element_type=F32)
            s = s * SCALE + jnp.where(valid, (-slopes[h]) * dist, NEG_INF)
            sink = sink_ref[h]
            m = jnp.maximum(jnp.max(s, axis=-1, keepdims=True), sink)
            p = jnp.exp(s - m)
            l = jnp.sum(p, axis=-1, keepdims=True) + jnp.exp(sink - m)
            o = jnp.dot(p.astype(BF16), v, preferred_element_type=F32) / l
            o_ref[:, hc] = (o * _silu(g_ref[:, hc].astype(F32))).astype(BF16)


def _attn_a(proj, sink):
    nq = SEQ // A_TQ
    r128 = A_TQ // A_WINDOW
    nb128 = SEQ // A_WINDOW
    kcol = (MIX_WIDTH + A_Q) // A_KV
    vcol = kcol + 1

    def halo(col, which):
        if which == "prev":
            return pl.BlockSpec((A_WINDOW, A_KV),
                                lambda b, n: (b * nb128 + jnp.maximum(n * r128 - 1, 0), col))
        if which == "next":
            return pl.BlockSpec((A_WINDOW, A_KV),
                                lambda b, n: (b * nb128 + jnp.minimum((n + 1) * r128, nb128 - 1), col))
        return pl.BlockSpec((A_TQ, A_KV), lambda b, n: (b * nq + n, col))

    return pl.pallas_call(
        _attn_a_kernel,
        grid=(BATCH, nq),
        in_specs=[pl.BlockSpec(memory_space=pltpu.SMEM),
                  pl.BlockSpec((A_TQ, A_Q), lambda b, n: (b * nq + n, 1)),
                  halo(kcol, "prev"), halo(kcol, "main"), halo(kcol, "next"),
                  halo(vcol, "prev"), halo(vcol, "main"), halo(vcol, "next"),
                  pl.BlockSpec((A_TQ, MIX_WIDTH), lambda b, n: (b * nq + n, 0))],
        out_specs=pl.BlockSpec((A_TQ, MIX_WIDTH), lambda b, n: (b * nq + n, 0)),
        out_shape=jax.ShapeDtypeStruct((TOKENS, MIX_WIDTH), BF16),
        compiler_params=_params("arbitrary", "arbitrary"),
        name="attn_window",
    )(sink, proj, proj, proj, proj, proj, proj, proj, proj)


NB_KEYS = NB_WIN_H * GRID_W


def _rpb_table_kernel(rpb_ref, out_ref):
    h = pl.program_id(0)
    wq = lax.broadcasted_iota(jnp.int32, (GRID_W, GRID_W), 0)
    wk = lax.broadcasted_iota(jnp.int32, (GRID_W, GRID_W), 1)
    dx = jnp.clip(wk - wq + NB_WIN_W - 1, 0, NB_DX - 1)
    cs = jnp.clip(wq - NB_WIN_W // 2, 0, GRID_W - NB_WIN_W)
    col_valid = (wk >= cs) & (wk < cs + NB_WIN_W)
    base = h * (NB_DY * NB_DX)
    tiles = []
    for dy in range(NB_DY):
        acc = jnp.zeros((GRID_W, GRID_W), F32)
        for d in range(NB_DX):
            acc = jnp.where(dx == d, rpb_ref[base + dy * NB_DX + d], acc)
        tiles.append(jnp.where(col_valid, acc, NEG_INF))
    for v in range(NB_WIN_H):
        out_ref[0, v] = jnp.concatenate([tiles[v + y] for y in range(NB_WIN_H)], axis=-1)


def _rpb_table(rpb):
    return pl.pallas_call(
        _rpb_table_kernel,
        grid=(B_HEADS,),
        in_specs=[pl.BlockSpec(memory_space=pltpu.SMEM)],
        out_specs=pl.BlockSpec((1, NB_WIN_H, GRID_W, NB_KEYS), lambda h: (h, 0, 0, 0)),
        out_shape=jax.ShapeDtypeStruct((B_HEADS, NB_WIN_H, GRID_W, NB_KEYS), F32),
        compiler_params=_params("arbitrary"),
        name="rpb_table",
    )(rpb.reshape(-1))


def _attn_b_kernel(tbl_ref, q_ref, k_ref, v_ref, g_ref, o_ref):
    def row(r, carry):
        rs = jnp.clip(r - NB_WIN_H // 2, 0, GRID_ROWS - NB_WIN_H)
        dy0 = rs - r + NB_WIN_H - 1
        q0 = pl.multiple_of(r * GRID_W, GRID_W)
        k0 = pl.multiple_of(rs * GRID_W, GRID_W)
        q = q_ref[pl.ds(q0, GRID_W), :]
        k = k_ref[pl.ds(k0, NB_KEYS), :]
        v = v_ref[pl.ds(k0, NB_KEYS), :]
        s = lax.dot_general(q, k, (((1,), (1,)), ((), ())), preferred_element_type=F32)
        s = s * SCALE + tbl_ref[0, dy0]
        m = jnp.max(s, axis=-1, keepdims=True)
        p = jnp.exp(s - m)
        l = jnp.sum(p, axis=-1, keepdims=True)
        o = jnp.dot(p.astype(BF16), v, preferred_element_type=F32) / l
        g = g_ref[pl.ds(q0, GRID_W), :].astype(F32)
        o_ref[pl.ds(q0, GRID_W), :] = (o * _silu(g)).astype(BF16)
        return carry

    lax.fori_loop(0, GRID_ROWS, row, 0)


def _attn_b(proj, tbl):
    def head(off):
        return pl.BlockSpec((SEQ, HEAD_DIM), lambda h, b: (b, off + h))

    return pl.pallas_call(
        _attn_b_kernel,
        grid=(B_HEADS, BATCH),
        in_specs=[pl.BlockSpec((1, NB_WIN_H, GRID_W, NB_KEYS), lambda h, b: (h, 0, 0, 0)),
                  head(0), head(B_HEADS), head(2 * B_HEADS), head(3 * B_HEADS)],
        out_specs=head(0),
        out_shape=jax.ShapeDtypeStruct((TOKENS, MIX_WIDTH), BF16),
        compiler_params=_params("arbitrary", "arbitrary"),
        name="attn_neighbourhood",
    )(tbl, proj, proj, proj, proj)


C_TQ = 256


def _attn_c_kernel(slope_ref, lam_ref, subln_ref, q_ref, k_ref, v_ref, g_ref, o_ref, *, lambda_init):
    h = pl.program_id(1)
    n = pl.program_id(2)
    lp = lam_ref[...]
    la = jnp.sum(lp[0:1] * lp[1:2], axis=-1, keepdims=True)
    lb = jnp.sum(lp[2:3] * lp[3:4], axis=-1, keepdims=True)
    lam = jnp.exp(la) - jnp.exp(lb) + lambda_init
    qpos = n * C_TQ + lax.broadcasted_iota(jnp.int32, (C_TQ, SEQ), 0)
    kpos = lax.broadcasted_iota(jnp.int32, (C_TQ, SEQ), 1)
    bias = (-slope_ref[h]) * jnp.abs(qpos - kpos).astype(F32)
    probs = []
    for mp in range(2):
        c = slice(mp * HEAD_DIM, (mp + 1) * HEAD_DIM)
        s = lax.dot_general(q_ref[:, c], k_ref[:, c], (((1,), (1,)), ((), ())),
                            preferred_element_type=F32)
        s = s * SCALE + bias
        m = jnp.max(s, axis=-1, keepdims=True)
        p = jnp.exp(s - m)
        l = jnp.sum(p, axis=-1, keepdims=True)
        probs.append(p * (1.0 / l))
    pd = probs[0] - lam * probs[1]
    o = jnp.dot(pd.astype(BF16), v_ref[...], preferred_element_type=F32)
    o = _rms(o, subln_ref[...]) * (1.0 - lambda_init)
    o_ref[...] = (o * _silu(g_ref[...].astype(F32))).astype(BF16)


def _attn_c(proj, lam_p, subln, lambda_init):
    nq = SEQ // C_TQ
    slopes = jnp.asarray(np.array(_alibi_slopes(C_HEADS), dtype=np.float32))

    def tile(off):
        return pl.BlockSpec((C_TQ, C_V_DIM), lambda b, h, n: (b * nq + n, off + h))

    def full(off):
        return pl.BlockSpec((SEQ, C_V_DIM), lambda b, h, n: (b, off + h))

    return pl.pallas_call(
        functools.partial(_attn_c_kernel, lambda_init=lambda_init),
        grid=(BATCH, C_HEADS, nq),
        in_specs=[pl.BlockSpec(memory_space=pltpu.SMEM),
                  pl.BlockSpec((4, HEAD_DIM), lambda b, h, n: (0, 0)),
                  pl.BlockSpec((1, C_V_DIM), lambda b, h, n: (0, 0)),
                  tile(0), full(C_HEADS), full(2 * C_HEADS), tile(3 * C_HEADS)],
        out_specs=tile(0),
        out_shape=jax.ShapeDtypeStruct((TOKENS, MIX_WIDTH), BF16),
        compiler_params=_params("arbitrary", "arbitrary", "arbitrary"),
        name="attn_diff",
    )(slopes, lam_p, subln.reshape(1, C_V_DIM), proj, proj, proj, proj)


TAIL_TM = 256


def _tail_kernel(og_ref, x_ref, p_ref, wo_ref, wg_ref, wp_ref, npost_ref, npre_ref, xo_ref, *h_ref):
    y = jnp.dot(og_ref[...], wo_ref[...], preferred_element_type=F32)
    x1 = x_ref[...] + _rms(y, npost_ref[...])
    gate = jax.nn.sigmoid(jnp.dot(x1.astype(BF16), wg_ref[...], preferred_element_type=F32))
    pe = jnp.dot(p_ref[...].astype(BF16), wp_ref[...], preferred_element_type=F32)
    x2 = x1 + gate * pe
    xo_ref[...] = x2
    if h_ref:
        h_ref[0][...] = _rms(x2, npre_ref[...]).astype(BF16)


def _tail(og, x2d, p2d, wo, wg, wp, npost, npre_next):
    tm = TAIL_TM
    with_h = npre_next is not None
    if not with_h:
        npre_next = npost
    row = lambda w: pl.BlockSpec((tm, w), lambda i: (i, 0))
    const = lambda a, b: pl.BlockSpec((a, b), lambda i: (0, 0), pipeline_mode=pl.Buffered(1))
    out_specs = [row(D_MODEL)]
    out_shape = [jax.ShapeDtypeStruct((TOKENS, D_MODEL), F32)]
    if with_h:
        out_specs.append(row(D_MODEL))
        out_shape.append(jax.ShapeDtypeStruct((TOKENS, D_MODEL), BF16))
    res = pl.pallas_call(
        _tail_kernel,
        grid=(TOKENS // tm,),
        in_specs=[row(MIX_WIDTH), row(D_MODEL), row(PE_DIM),
                  const(MIX_WIDTH, D_MODEL), const(D_MODEL, D_MODEL), const(PE_DIM, D_MODEL),
                  const(1, D_MODEL), const(1, D_MODEL)],
        out_specs=out_specs,
        out_shape=out_shape,
        compiler_params=_params("arbitrary"),
        name="tail",
    )(og, x2d, p2d, wo, wg, wp, npost, npre_next)
    return (res[0], res[1]) if with_h else (res[0], None)


def kernel(x, p, norm_pre, norm_post, w_out, pe_proj, pe_gate, a_w_in, a_sink, b_w_in, b_rpb,
           c_w_in, c_lambda, c_subln):
    x2d = x.reshape(TOKENS, D_MODEL)
    p3d = p.reshape(DEPTH, TOKENS, PE_DIM)
    row = lambda a: a.reshape(1, D_MODEL)
    h = _prenorm(x2d, row(norm_pre[0]))
    for i in range(DEPTH):
        kind, j = i % N_MIXERS, i // N_MIXERS
        if kind == 0:
            w = a_w_in[j]
            split = A_Q + 2 * A_KV
            w = jnp.concatenate([w[:, split:], w[:, :split]], axis=1).astype(BF16)
            og = _attn_a(_in_proj(h, w), a_sink[j])
        elif kind == 1:
            og = _attn_b(_in_proj(h, b_w_in[j].astype(BF16)), _rpb_table(b_rpb[j]))
        else:
            lambda_init = 0.8 - 0.6 * math.exp(-0.3 * i)
            og = _attn_c(_in_proj(h, c_w_in[j].astype(BF16)), c_lambda[j], c_subln[j], lambda_init)
        npre_next = row(norm_pre[i + 1]) if i + 1 < DEPTH else None
        x2d, h = _tail(og, x2d, p3d[i], w_out[i].astype(BF16), pe_gate[i].astype(BF16),
                       pe_proj[i].astype(BF16), row(norm_post[i]), npre_next)
    return x2d.reshape(BATCH, SEQ, D_MODEL)
```

```python
import functools
import math

import jax
import jax.numpy as jnp
import numpy as np
from jax import lax
from jax.experimental import pallas as pl
from jax.experimental.pallas import tpu as pltpu

F32 = jnp.float32
BF16 = jnp.bfloat16

D_MODEL = 2048
BATCH = 4
SEQ = 2048
DEPTH = 4
TOKENS = BATCH * SEQ
PE_DIM = 256
GRID_W = 64
GRID_ROWS = SEQ // GRID_W
N_MIXERS = 3
HEAD_DIM = 128
MIX_WIDTH = D_MODEL
EPS = 1e-6
NEG_INF = -1e30
LOG2E = math.log2(math.e)
Q_SCALE = HEAD_DIM ** -0.5 * LOG2E

A_HEADS = 16
A_KV_HEADS = 4
A_GROUP = A_HEADS // A_KV_HEADS
A_WINDOW = 128
A_Q = A_HEADS * HEAD_DIM
A_KV = A_KV_HEADS * HEAD_DIM
A_IN = A_Q + 2 * A_KV + MIX_WIDTH

B_HEADS = 16
NB_WIN_H = 8
NB_WIN_W = 16
NB_DY = 2 * NB_WIN_H - 1
NB_DX = 2 * NB_WIN_W - 1
B_IN = 3 * B_HEADS * HEAD_DIM + MIX_WIDTH

C_HEADS = 8
C_V_DIM = 2 * HEAD_DIM
C_IN = 4 * MIX_WIDTH

VMEM_LIMIT_BYTES = 52 * 1024 * 1024


def _alibi_slopes(n_heads):
    return [2.0 ** (-8.0 * (h + 1) / n_heads) for h in range(n_heads)]


def _params(*sem):
    return pltpu.CompilerParams(dimension_semantics=sem, vmem_limit_bytes=VMEM_LIMIT_BYTES)


def _rms(xf, w):
    ms = jnp.mean(xf * xf, axis=-1, keepdims=True)
    return xf * lax.rsqrt(ms + EPS) * w


def _silu(g):
    return g * jax.nn.sigmoid(g)


def _dot_nt(a, b):
    return lax.dot_general(a, b, (((1,), (1,)), ((), ())), preferred_element_type=F32)


def _prenorm_kernel(x_ref, w_ref, h_ref):
    h_ref[...] = _rms(x_ref[...], w_ref[...]).astype(BF16)


def _prenorm(x2d, norm_pre3):
    tm = 512
    return pl.pallas_call(
        _prenorm_kernel,
        grid=(TOKENS // tm,),
        in_specs=[pl.BlockSpec((tm, D_MODEL), lambda i: (i, 0)),
                  pl.BlockSpec((None, 1, D_MODEL), lambda i: (0, 0, 0))],
        out_specs=pl.BlockSpec((tm, D_MODEL), lambda i: (i, 0)),
        out_shape=jax.ShapeDtypeStruct((TOKENS, D_MODEL), BF16),
        compiler_params=_params("arbitrary"),
        name="prenorm",
    )(x2d, norm_pre3)


PROJ_TM = 1024
PROJ_TN = 1024


def _in_proj_kernel(h_ref, w_ref, cs_ref, o_ref, wb_ref):
    @pl.when(pl.program_id(1) == 0)
    def _():
        wb_ref[...] = w_ref[...].astype(BF16)

    acc = jnp.dot(h_ref[...], wb_ref[...], preferred_element_type=F32)
    o_ref[...] = (acc * cs_ref[...]).astype(o_ref.dtype)


def _in_proj(h, w_all, layer_j, colscale, out_tile):
    n = w_all.shape[2]
    return pl.pallas_call(
        _in_proj_kernel,
        grid=(n // PROJ_TN, TOKENS // PROJ_TM),
        in_specs=[pl.BlockSpec((PROJ_TM, D_MODEL), lambda j, i: (i, 0)),
                  pl.BlockSpec((None, D_MODEL, PROJ_TN), lambda j, i: (layer_j, 0, j)),
                  pl.BlockSpec((1, PROJ_TN), lambda j, i: (0, j))],
        out_specs=pl.BlockSpec((PROJ_TM, PROJ_TN), lambda j, i: (i, out_tile(j))),
        out_shape=jax.ShapeDtypeStruct((TOKENS, n), BF16),
        scratch_shapes=[pltpu.VMEM((D_MODEL, PROJ_TN), BF16)],
        compiler_params=_params("arbitrary", "arbitrary"),
        name="in_proj",
    )(h, w_all, colscale)


def _colscale(n, n_q):
    cs = np.ones((1, n), np.float32)
    cs[:, :n_q] = Q_SCALE
    return jnp.asarray(cs)


A_TQ = 256
A_KEYS = 3 * A_WINDOW


def _attn_a_kernel(sink_ref, q_ref, kp_ref, km_ref, kn_ref, vp_ref, vm_ref, vn_ref, g_ref, o_ref):
    n = pl.program_id(1)
    nsub = A_TQ // A_WINDOW
    qi = lax.broadcasted_iota(jnp.int32, (A_WINDOW, A_WINDOW), 0)
    kr = lax.broadcasted_iota(jnp.int32, (A_WINDOW, A_WINDOW), 1)
    d_prev = (qi + A_WINDOW - kr).astype(F32)
    d_mid = jnp.abs(qi - kr).astype(F32)
    d_next = (kr + A_WINDOW - qi).astype(F32)
    slopes = _alibi_slopes(A_HEADS)
    kcat, vcat = [], []
    for kv in range(A_KV_HEADS):
        c = slice(kv * HEAD_DIM, (kv + 1) * HEAD_DIM)
        kcat.append(jnp.concatenate([kp_ref[:, c], km_ref[:, c], kn_ref[:, c]], axis=0))
        vcat.append(jnp.concatenate([vp_ref[:, c], vm_ref[:, c], vn_ref[:, c]], axis=0))

    def heads_of(kv):
        return [kv * A_GROUP + g for g in range(A_GROUP)]

    def scores(kv, j):
        rows = slice(j * A_WINDOW, (j + 1) * A_WINDOW)
        qs = jnp.concatenate([q_ref[rows, h * HEAD_DIM:(h + 1) * HEAD_DIM] for h in heads_of(kv)], axis=0)
        return _dot_nt(qs, kcat[kv][j * A_WINDOW:j * A_WINDOW + A_KEYS])

    def finish(kv, j, s):
        rows = slice(j * A_WINDOW, (j + 1) * A_WINDOW)
        off_prev = jnp.where(n > 0, 0, A_WINDOW) if j == 0 else 0
        off_next = jnp.where(n < SEQ // A_TQ - 1, 0, A_WINDOW) if j == nsub - 1 else 0
        ok_prev = kr >= qi + off_prev
        ok_next = kr <= qi - off_next
        probs, denoms = [], []
        for g, h in enumerate(heads_of(kv)):
            sg = s[g * A_WINDOW:(g + 1) * A_WINDOW]
            ch = slopes[h] * LOG2E
            s0 = jnp.where(ok_prev, sg[:, :A_WINDOW] - ch * d_prev, NEG_INF)
            s1 = sg[:, A_WINDOW:2 * A_WINDOW] - ch * d_mid
            s2 = jnp.where(ok_next, sg[:, 2 * A_WINDOW:] - ch * d_next, NEG_INF)
            sink = sink_ref[h] * LOG2E
            m = jnp.max(jnp.maximum(jnp.maximum(s0, s1), s2), axis=-1, keepdims=True)
            m = jnp.maximum(m, sink)
            p0, p1, p2 = jnp.exp2(s0 - m), jnp.exp2(s1 - m), jnp.exp2(s2 - m)
            denoms.append(jnp.sum(p0 + p1 + p2, axis=-1, keepdims=True) + jnp.exp2(sink - m))
            probs.append(jnp.concatenate([p0, p1, p2], axis=-1).astype(BF16))
        o = jnp.dot(jnp.concatenate(probs, axis=0), vcat[kv][j * A_WINDOW:j * A_WINDOW + A_KEYS],
                    preferred_element_type=F32)
        for g, h in enumerate(heads_of(kv)):
            hc = slice(h * HEAD_DIM, (h + 1) * HEAD_DIM)
            og = o[g * A_WINDOW:(g + 1) * A_WINDOW] / denoms[g]
            o_ref[rows, hc] = (og * _silu(g_ref[rows, hc].astype(F32))).astype(BF16)

    units = [(kv, j) for kv in range(A_KV_HEADS) for j in range(nsub)]
    pending = scores(*units[0])
    for u, unit in enumerate(units):
        current = pending
        if u + 1 < len(units):
            pending = scores(*units[u + 1])
        finish(*unit, current)


def _attn_a(proj, sink):
    nq = SEQ // A_TQ
    r128 = A_TQ // A_WINDOW
    nb128 = SEQ // A_WINDOW
    kcol = (MIX_WIDTH + A_Q) // A_KV
    vcol = kcol + 1

    def halo(col, which):
        if which == "prev":
            return pl.BlockSpec((A_WINDOW, A_KV),
                                lambda b, n: (b * nb128 + jnp.maximum(n * r128 - 1, 0), col))
        if which == "next":
            return pl.BlockSpec((A_WINDOW, A_KV),
                                lambda b, n: (b * nb128 + jnp.minimum((n + 1) * r128, nb128 - 1), col))
        return pl.BlockSpec((A_TQ, A_KV), lambda b, n: (b * nq + n, col))

    return pl.pallas_call(
        _attn_a_kernel,
        grid=(BATCH, nq),
        in_specs=[pl.BlockSpec(memory_space=pltpu.SMEM),
                  pl.BlockSpec((A_TQ, A_Q), lambda b, n: (b * nq + n, 1)),
                  halo(kcol, "prev"), halo(kcol, "main"), halo(kcol, "next"),
                  halo(vcol, "prev"), halo(vcol, "main"), halo(vcol, "next"),
                  pl.BlockSpec((A_TQ, MIX_WIDTH), lambda b, n: (b * nq + n, 0))],
        out_specs=pl.BlockSpec((A_TQ, MIX_WIDTH), lambda b, n: (b * nq + n, 0)),
        out_shape=jax.ShapeDtypeStruct((TOKENS, MIX_WIDTH), BF16),
        compiler_params=_params("arbitrary", "arbitrary"),
        name="attn_window",
    )(sink, proj, proj, proj, proj, proj, proj, proj, proj)


NB_KEYS = NB_WIN_H * GRID_W


def _rpb_table_kernel(rpb_ref, out_ref):
    h = pl.program_id(0)
    wq = lax.broadcasted_iota(jnp.int32, (GRID_W, GRID_W), 0)
    wk = lax.broadcasted_iota(jnp.int32, (GRID_W, GRID_W), 1)
    dx = jnp.clip(wk - wq + NB_WIN_W - 1, 0, NB_DX - 1)
    cs = jnp.clip(wq - NB_WIN_W // 2, 0, GRID_W - NB_WIN_W)
    col_valid = (wk >= cs) & (wk < cs + NB_WIN_W)
    base = h * (NB_DY * NB_DX)
    tiles = []
    for dy in range(NB_DY):
        acc = jnp.zeros((GRID_W, GRID_W), F32)
        for d in range(NB_DX):
            acc = jnp.where(dx == d, rpb_ref[base + dy * NB_DX + d], acc)
        tiles.append(jnp.where(col_valid, acc * LOG2E, NEG_INF))
    for v in range(NB_WIN_H):
        out_ref[0, v] = jnp.concatenate([tiles[v + y] for y in range(NB_WIN_H)], axis=-1)


def _rpb_table(rpb_all, layer_j):
    per_layer = B_HEADS * NB_DY * NB_DX
    return pl.pallas_call(
        _rpb_table_kernel,
        grid=(B_HEADS,),
        in_specs=[pl.BlockSpec(memory_space=pltpu.SMEM)],
        out_specs=pl.BlockSpec((1, NB_WIN_H, GRID_W, NB_KEYS), lambda h: (h, 0, 0, 0)),
        out_shape=jax.ShapeDtypeStruct((B_HEADS, NB_WIN_H, GRID_W, NB_KEYS), F32),
        compiler_params=_params("arbitrary"),
        name="rpb_table",
    )(rpb_all.reshape(-1)[layer_j * per_layer:(layer_j + 1) * per_layer])


NB_GROUP = 4


def _attn_b_kernel(tbl_ref, q_ref, k_ref, v_ref, g_ref, o_ref):
    def window(r):
        rs = min(max(r - NB_WIN_H // 2, 0), GRID_ROWS - NB_WIN_H)
        return rs, rs - r + NB_WIN_H - 1

    def scores(r):
        rs, dy0 = window(r)
        q = q_ref[r * GRID_W:(r + 1) * GRID_W, :]
        k = k_ref[rs * GRID_W:rs * GRID_W + NB_KEYS, :]
        return _dot_nt(q, k) + tbl_ref[0, dy0]

    def finish(r, s):
        rs, _ = window(r)
        rows = slice(r * GRID_W, (r + 1) * GRID_W)
        m = jnp.max(s, axis=-1, keepdims=True)
        p = jnp.exp2(s - m)
        l = jnp.sum(p, axis=-1, keepdims=True)
        v = v_ref[rs * GRID_W:rs * GRID_W + NB_KEYS, :]
        o = jnp.dot(p.astype(BF16), v, preferred_element_type=F32) / l
        o_ref[rows, :] = (o * _silu(g_ref[rows, :].astype(F32))).astype(BF16)

    groups = [range(r, r + NB_GROUP) for r in range(0, GRID_ROWS, NB_GROUP)]
    pending = [scores(r) for r in groups[0]]
    for gi, grp in enumerate(groups):
        current = pending
        if gi + 1 < len(groups):
            pending = [scores(r) for r in groups[gi + 1]]
        for r, s in zip(grp, current):
            finish(r, s)


def _attn_b(proj, tbl):
    def head(off):
        return pl.BlockSpec((SEQ, HEAD_DIM), lambda h, b: (b, off + h))

    return pl.pallas_call(
        _attn_b_kernel,
        grid=(B_HEADS, BATCH),
        in_specs=[pl.BlockSpec((1, NB_WIN_H, GRID_W, NB_KEYS), lambda h, b: (h, 0, 0, 0)),
                  head(0), head(B_HEADS), head(2 * B_HEADS), head(3 * B_HEADS)],
        out_specs=head(0),
        out_shape=jax.ShapeDtypeStruct((TOKENS, MIX_WIDTH), BF16),
        compiler_params=_params("arbitrary", "arbitrary"),
        name="attn_neighbourhood",
    )(tbl, proj, proj, proj, proj)


C_TQ = 256
C_NT = SEQ // C_TQ
C_ND = 2 * C_NT - 1


def _alibi_table_kernel(slope_ref, out_ref):
    h = pl.program_id(0)
    ii = lax.broadcasted_iota(jnp.int32, (C_TQ, C_TQ), 0)
    jj = lax.broadcasted_iota(jnp.int32, (C_TQ, C_TQ), 1)
    for d in range(C_ND):
        dist = jnp.abs((C_NT - 1 - d) * C_TQ + ii - jj).astype(F32)
        out_ref[0, d] = (-slope_ref[h] * LOG2E) * dist


def _alibi_table():
    slopes = jnp.asarray(np.array(_alibi_slopes(C_HEADS), dtype=np.float32))
    return pl.pallas_call(
        _alibi_table_kernel,
        grid=(C_HEADS,),
        in_specs=[pl.BlockSpec(memory_space=pltpu.SMEM)],
        out_specs=pl.BlockSpec((1, C_ND, C_TQ, C_TQ), lambda h: (h, 0, 0, 0)),
        out_shape=jax.ShapeDtypeStruct((C_HEADS, C_ND, C_TQ, C_TQ), F32),
        compiler_params=_params("arbitrary"),
        name="alibi_table",
    )(slopes)


def _attn_c_kernel(lam_ref, subln_ref, tbl_ref, q_ref, k_ref, v_ref, g_ref, o_ref, s_scr, pd_scr,
                   *, lambda_init):
    lp = lam_ref[...]
    la = jnp.sum(lp[0:1] * lp[1:2], axis=-1, keepdims=True)
    lb = jnp.sum(lp[2:3] * lp[3:4], axis=-1, keepdims=True)
    lam = jnp.exp(la) - jnp.exp(lb) + lambda_init
    maps = [slice(mp * HEAD_DIM, (mp + 1) * HEAD_DIM) for mp in range(2)]
    tiles = [slice(c * C_TQ, (c + 1) * C_TQ) for c in range(C_NT)]

    def scores(n, slot):
        rows = tiles[n]
        s = [_dot_nt(q_ref[rows, mc], k_ref[:, mc]) for mc in maps]
        mx = [None, None]
        for c, tc in enumerate(tiles):
            bias = tbl_ref[0, c - n + C_NT - 1]
            for mp in range(2):
                x = s[mp][:, tc] + bias
                s_scr[slot, mp, :, tc] = x
                mx[mp] = x if mx[mp] is None else jnp.maximum(mx[mp], x)
        return [jnp.max(x, axis=-1, keepdims=True) for x in mx]

    def finish(n, slot, m):
        rows = tiles[n]
        acc = [None, None]
        for tc in tiles:
            for mp in range(2):
                p = jnp.exp2(s_scr[slot, mp, :, tc] - m[mp])
                s_scr[slot, mp, :, tc] = p
                acc[mp] = p if acc[mp] is None else acc[mp] + p
        l = [jnp.sum(x, axis=-1, keepdims=True) for x in acc]
        w0 = 1.0 / l[0]
        w1 = lam / l[1]
        for tc in tiles:
            pd_scr[slot, :, tc] = (s_scr[slot, 0, :, tc] * w0 - s_scr[slot, 1, :, tc] * w1).astype(BF16)
        o = jnp.dot(pd_scr[slot], v_ref[...], preferred_element_type=F32)
        o = _rms(o, subln_ref[...]) * (1.0 - lambda_init)
        o_ref[rows, :] = (o * _silu(g_ref[rows, :].astype(F32))).astype(BF16)

    m_next = scores(0, 0)
    for n in range(C_NT):
        m_cur = m_next
        if n + 1 < C_NT:
            m_next = scores(n + 1, (n + 1) % 2)
        finish(n, n % 2, m_cur)


def _attn_c(proj, tbl, lam_all, subln_all, layer_j, lambda_init):
    def head(off):
        return pl.BlockSpec((SEQ, C_V_DIM), lambda b, h: (b, off + h))

    return pl.pallas_call(
        functools.partial(_attn_c_kernel, lambda_init=lambda_init),
        grid=(BATCH, C_HEADS),
        in_specs=[pl.BlockSpec((None, 4, HEAD_DIM), lambda b, h: (layer_j, 0, 0)),
                  pl.BlockSpec((None, 1, C_V_DIM), lambda b, h: (layer_j, 0, 0)),
                  pl.BlockSpec((1, C_ND, C_TQ, C_TQ), lambda b, h: (h, 0, 0, 0)),
                  head(0), head(C_HEADS), head(2 * C_HEADS), head(3 * C_HEADS)],
        out_specs=head(0),
        out_shape=jax.ShapeDtypeStruct((TOKENS, MIX_WIDTH), BF16),
        scratch_shapes=[pltpu.VMEM((2, 2, C_TQ, SEQ), F32), pltpu.VMEM((2, C_TQ, SEQ), BF16)],
        compiler_params=_params("arbitrary", "arbitrary"),
        name="attn_diff",
    )(lam_all, subln_all.reshape(-1, 1, C_V_DIM), tbl, proj, proj, proj, proj)


TAIL_TM = 256


def _tail_kernel(*refs, with_h):
    og_ref, x_ref, p_ref, wo_ref, wg_ref, wp_ref, npost_ref = refs[:7]
    y = jnp.dot(og_ref[...], wo_ref[...], preferred_element_type=F32)
    x1 = x_ref[...] + _rms(y, npost_ref[...])
    gate = jax.nn.sigmoid(jnp.dot(x1.astype(BF16), wg_ref[...], preferred_element_type=F32))
    pe = jnp.dot(p_ref[...].astype(BF16), wp_ref[...], preferred_element_type=F32)
    x2 = x1 + gate * pe
    if with_h:
        npre_ref, xo_ref, h_ref = refs[7:]
        h_ref[...] = _rms(x2, npre_ref[...]).astype(BF16)
    else:
        xo_ref, = refs[7:]
    xo_ref[...] = x2


def _tail(og, x2d, p3d, wo, wg, wp, norm_post3, norm_pre3, layer):
    tm = TAIL_TM
    with_h = layer + 1 < DEPTH
    row = lambda w: pl.BlockSpec((tm, w), lambda i: (i, 0))
    const = lambda a, b, l: pl.BlockSpec((None, a, b), lambda i: (l, 0, 0), pipeline_mode=pl.Buffered(1))
    in_specs = [row(MIX_WIDTH), row(D_MODEL),
                pl.BlockSpec((None, tm, PE_DIM), lambda i: (layer, i, 0)),
                const(MIX_WIDTH, D_MODEL, layer), const(D_MODEL, D_MODEL, layer),
                const(PE_DIM, D_MODEL, layer), const(1, D_MODEL, layer)]
    args = [og, x2d, p3d, wo, wg, wp, norm_post3]
    out_specs = [row(D_MODEL)]
    out_shape = [jax.ShapeDtypeStruct((TOKENS, D_MODEL), F32)]
    if with_h:
        in_specs.append(const(1, D_MODEL, layer + 1))
        args.append(norm_pre3)
        out_specs.append(row(D_MODEL))
        out_shape.append(jax.ShapeDtypeStruct((TOKENS, D_MODEL), BF16))
    res = pl.pallas_call(
        functools.partial(_tail_kernel, with_h=with_h),
        grid=(TOKENS // tm,),
        in_specs=in_specs,
        out_specs=out_specs,
        out_shape=out_shape,
        compiler_params=_params("arbitrary"),
        name="tail",
    )(*args)
    return (res[0], res[1]) if with_h else (res[0], None)


def kernel(x, p, norm_pre, norm_post, w_out, pe_proj, pe_gate, a_w_in, a_sink, b_w_in, b_rpb,
           c_w_in, c_lambda, c_subln):
    x2d = x.reshape(TOKENS, D_MODEL)
    p3d = p.reshape(DEPTH, TOKENS, PE_DIM)
    norm_pre3 = norm_pre.reshape(DEPTH, 1, D_MODEL)
    norm_post3 = norm_post.reshape(DEPTH, 1, D_MODEL)
    wo, wg, wp = w_out.astype(BF16), pe_gate.astype(BF16), pe_proj.astype(BF16)
    a_tiles = A_IN // PROJ_TN
    g_tiles = MIX_WIDTH // PROJ_TN
    h = _prenorm(x2d, norm_pre3)
    for i in range(DEPTH):
        kind, j = i % N_MIXERS, i // N_MIXERS
        if kind == 0:
            proj = _in_proj(h, a_w_in, j, _colscale(A_IN, A_Q), lambda t: (t + g_tiles) % a_tiles)
            og = _attn_a(proj, a_sink[j])
        elif kind == 1:
            proj = _in_proj(h, b_w_in, j, _colscale(B_IN, MIX_WIDTH), lambda t: t)
            og = _attn_b(proj, _rpb_table(b_rpb, j))
        else:
            lambda_init = 0.8 - 0.6 * math.exp(-0.3 * i)
            proj = _in_proj(h, c_w_in, j, _colscale(C_IN, MIX_WIDTH), lambda t: t)
            og = _attn_c(proj, _alibi_table(), c_lambda, c_subln, j, lambda_init)
        x2d, h = _tail(og, x2d, p3d, wo, wg, wp, norm_post3, norm_pre3, i)
    return x2d.reshape(BATCH, SEQ, D_MODEL)
```

```python
import functools
import math

import jax
import jax.numpy as jnp
import numpy as np
from jax import lax
from jax.experimental import pallas as pl
from jax.experimental.pallas import tpu as pltpu

F32 = jnp.float32
BF16 = jnp.bfloat16

D_MODEL = 2048
BATCH = 4
SEQ = 2048
DEPTH = 4
TOKENS = BATCH * SEQ
PE_DIM = 256
GRID_W = 64
GRID_ROWS = SEQ // GRID_W
N_MIXERS = 3
HEAD_DIM = 128
MIX_WIDTH = D_MODEL
EPS = 1e-6
NEG_INF = -1e30
LOG2E = math.log2(math.e)
Q_SCALE = HEAD_DIM ** -0.5 * LOG2E

A_HEADS = 16
A_KV_HEADS = 4
A_GROUP = A_HEADS // A_KV_HEADS
A_WINDOW = 128
A_Q = A_HEADS * HEAD_DIM
A_KV = A_KV_HEADS * HEAD_DIM
A_IN = A_Q + 2 * A_KV + MIX_WIDTH

B_HEADS = 16
NB_WIN_H = 8
NB_WIN_W = 16
NB_DY = 2 * NB_WIN_H - 1
NB_DX = 2 * NB_WIN_W - 1
B_IN = 3 * B_HEADS * HEAD_DIM + MIX_WIDTH

C_HEADS = 8
C_V_DIM = 2 * HEAD_DIM
C_IN = 4 * MIX_WIDTH

VMEM_LIMIT_BYTES = 52 * 1024 * 1024


def _alibi_slopes(n_heads):
    return [2.0 ** (-8.0 * (h + 1) / n_heads) for h in range(n_heads)]


def _params(*sem):
    return pltpu.CompilerParams(dimension_semantics=sem, vmem_limit_bytes=VMEM_LIMIT_BYTES)


def _rms(xf, w):
    ms = jnp.mean(xf * xf, axis=-1, keepdims=True)
    return xf * lax.rsqrt(ms + EPS) * w


def _silu(g):
    return g * jax.nn.sigmoid(g)


def _dot_nt(a, b):
    return lax.dot_general(a, b, (((1,), (1,)), ((), ())), preferred_element_type=F32)


def _prenorm_kernel(x_ref, w_ref, h_ref):
    h_ref[...] = _rms(x_ref[...], w_ref[...]).astype(BF16)


def _prenorm(x2d, norm_pre3):
    tm = 512
    return pl.pallas_call(
        _prenorm_kernel,
        grid=(TOKENS // tm,),
        in_specs=[pl.BlockSpec((tm, D_MODEL), lambda i: (i, 0)),
                  pl.BlockSpec((None, 1, D_MODEL), lambda i: (0, 0, 0))],
        out_specs=pl.BlockSpec((tm, D_MODEL), lambda i: (i, 0)),
        out_shape=jax.ShapeDtypeStruct((TOKENS, D_MODEL), BF16),
        compiler_params=_params("arbitrary"),
        name="prenorm",
    )(x2d, norm_pre3)


PROJ_TM = 1024
PROJ_TN = 1024


def _in_proj_kernel(h_ref, w_ref, cs_ref, o_ref, wb_ref):
    @pl.when(pl.program_id(1) == 0)
    def _():
        wb_ref[...] = w_ref[...].astype(BF16)

    acc = jnp.dot(h_ref[...], wb_ref[...], preferred_element_type=F32)
    o_ref[...] = (acc * cs_ref[...]).astype(o_ref.dtype)


def _in_proj(h, w_all, layer_j, colscale, out_tile):
    n = w_all.shape[2]
    return pl.pallas_call(
        _in_proj_kernel,
        grid=(n // PROJ_TN, TOKENS // PROJ_TM),
        in_specs=[pl.BlockSpec((PROJ_TM, D_MODEL), lambda j, i: (i, 0)),
                  pl.BlockSpec((None, D_MODEL, PROJ_TN), lambda j, i: (layer_j, 0, j)),
                  pl.BlockSpec((1, PROJ_TN), lambda j, i: (0, j))],
        out_specs=pl.BlockSpec((PROJ_TM, PROJ_TN), lambda j, i: (i, out_tile(j))),
        out_shape=jax.ShapeDtypeStruct((TOKENS, n), BF16),
        scratch_shapes=[pltpu.VMEM((D_MODEL, PROJ_TN), BF16)],
        compiler_params=_params("arbitrary", "arbitrary"),
        name="in_proj",
    )(h, w_all, colscale)


def _colscale(n, n_q):
    cs = np.ones((1, n), np.float32)
    cs[:, :n_q] = Q_SCALE
    return jnp.asarray(cs)


def _rider_specs(layer, nsteps, step_of):
    rows = D_MODEL // nsteps
    slab_in = pl.BlockSpec((None, rows, D_MODEL), lambda *g: (layer, step_of(*g), 0))
    slab_out = pl.BlockSpec((rows, D_MODEL), lambda *g: (step_of(*g), 0))
    in_specs = [slab_in, slab_in, pl.BlockSpec((None, PE_DIM, D_MODEL), lambda *g: (layer, 0, 0))]
    out_specs = [slab_out, slab_out, pl.BlockSpec((PE_DIM, D_MODEL), lambda *g: (0, 0))]
    out_shape = [jax.ShapeDtypeStruct((D_MODEL, D_MODEL), BF16), jax.ShapeDtypeStruct((D_MODEL, D_MODEL), BF16),
                 jax.ShapeDtypeStruct((PE_DIM, D_MODEL), BF16)]
    return in_specs, out_specs, out_shape


def _rider_body(step, wo_f, wg_f, wp_f, wo_b, wg_b, wp_b):
    wo_b[...] = wo_f[...].astype(BF16)
    wg_b[...] = wg_f[...].astype(BF16)

    @pl.when(step == 0)
    def _():
        wp_b[...] = wp_f[...].astype(BF16)


A_TQ = 256
A_KEYS = 3 * A_WINDOW


def _attn_a_kernel(sink_ref, q_ref, kp_ref, km_ref, kn_ref, vp_ref, vm_ref, vn_ref, g_ref, wo_f, wg_f, wp_f,
                   o_ref, wo_b, wg_b, wp_b):
    n = pl.program_id(1)
    _rider_body(pl.program_id(0) * (SEQ // A_TQ) + n, wo_f, wg_f, wp_f, wo_b, wg_b, wp_b)
    nsub = A_TQ // A_WINDOW
    qi = lax.broadcasted_iota(jnp.int32, (A_WINDOW, A_WINDOW), 0)
    kr = lax.broadcasted_iota(jnp.int32, (A_WINDOW, A_WINDOW), 1)
    d_prev = (qi + A_WINDOW - kr).astype(F32)
    d_mid = jnp.abs(qi - kr).astype(F32)
    d_next = (kr + A_WINDOW - qi).astype(F32)
    slopes = _alibi_slopes(A_HEADS)
    kcat, vcat = [], []
    for kv in range(A_KV_HEADS):
        c = slice(kv * HEAD_DIM, (kv + 1) * HEAD_DIM)
        kcat.append(jnp.concatenate([kp_ref[:, c], km_ref[:, c], kn_ref[:, c]], axis=0))
        vcat.append(jnp.concatenate([vp_ref[:, c], vm_ref[:, c], vn_ref[:, c]], axis=0))

    def heads_of(kv):
        return [kv * A_GROUP + g for g in range(A_GROUP)]

    def scores(kv, j):
        rows = slice(j * A_WINDOW, (j + 1) * A_WINDOW)
        qs = jnp.concatenate([q_ref[rows, h * HEAD_DIM:(h + 1) * HEAD_DIM] for h in heads_of(kv)], axis=0)
        return _dot_nt(qs, kcat[kv][j * A_WINDOW:j * A_WINDOW + A_KEYS])

    def finish(kv, j, s):
        rows = slice(j * A_WINDOW, (j + 1) * A_WINDOW)
        off_prev = jnp.where(n > 0, 0, A_WINDOW) if j == 0 else 0
        off_next = jnp.where(n < SEQ // A_TQ - 1, 0, A_WINDOW) if j == nsub - 1 else 0
        ok_prev = kr >= qi + off_prev
        ok_next = kr <= qi - off_next
        probs, denoms = [], []
        for g, h in enumerate(heads_of(kv)):
            sg = s[g * A_WINDOW:(g + 1) * A_WINDOW]
            ch = slopes[h] * LOG2E
            s0 = jnp.where(ok_prev, sg[:, :A_WINDOW] - ch * d_prev, NEG_INF)
            s1 = sg[:, A_WINDOW:2 * A_WINDOW] - ch * d_mid
            s2 = jnp.where(ok_next, sg[:, 2 * A_WINDOW:] - ch * d_next, NEG_INF)
            sink = sink_ref[h] * LOG2E
            m = jnp.max(jnp.maximum(jnp.maximum(s0, s1), s2), axis=-1, keepdims=True)
            m = jnp.maximum(m, sink)
            p0, p1, p2 = jnp.exp2(s0 - m), jnp.exp2(s1 - m), jnp.exp2(s2 - m)
            denoms.append(jnp.sum(p0 + p1 + p2, axis=-1, keepdims=True) + jnp.exp2(sink - m))
            probs.append(jnp.concatenate([p0, p1, p2], axis=-1).astype(BF16))
        o = jnp.dot(jnp.concatenate(probs, axis=0), vcat[kv][j * A_WINDOW:j * A_WINDOW + A_KEYS],
                    preferred_element_type=F32)
        for g, h in enumerate(heads_of(kv)):
            hc = slice(h * HEAD_DIM, (h + 1) * HEAD_DIM)
            og = o[g * A_WINDOW:(g + 1) * A_WINDOW] / denoms[g]
            o_ref[rows, hc] = (og * _silu(g_ref[rows, hc].astype(F32))).astype(BF16)

    units = [(kv, j) for kv in range(A_KV_HEADS) for j in range(nsub)]
    pending = scores(*units[0])
    for u, unit in enumerate(units):
        current = pending
        if u + 1 < len(units):
            pending = scores(*units[u + 1])
        finish(*unit, current)


def _attn_a(proj, sink, tail_w, layer):
    nq = SEQ // A_TQ
    r_in, r_out, r_shape = _rider_specs(layer, BATCH * nq, lambda b, n: b * nq + n)
    r128 = A_TQ // A_WINDOW
    nb128 = SEQ // A_WINDOW
    kcol = (MIX_WIDTH + A_Q) // A_KV
    vcol = kcol + 1

    def halo(col, which):
        if which == "prev":
            return pl.BlockSpec((A_WINDOW, A_KV),
                                lambda b, n: (b * nb128 + jnp.maximum(n * r128 - 1, 0), col))
        if which == "next":
            return pl.BlockSpec((A_WINDOW, A_KV),
                                lambda b, n: (b * nb128 + jnp.minimum((n + 1) * r128, nb128 - 1), col))
        return pl.BlockSpec((A_TQ, A_KV), lambda b, n: (b * nq + n, col))

    return pl.pallas_call(
        _attn_a_kernel,
        grid=(BATCH, nq),
        in_specs=[pl.BlockSpec(memory_space=pltpu.SMEM),
                  pl.BlockSpec((A_TQ, A_Q), lambda b, n: (b * nq + n, 1)),
                  halo(kcol, "prev"), halo(kcol, "main"), halo(kcol, "next"),
                  halo(vcol, "prev"), halo(vcol, "main"), halo(vcol, "next"),
                  pl.BlockSpec((A_TQ, MIX_WIDTH), lambda b, n: (b * nq + n, 0))] + r_in,
        out_specs=[pl.BlockSpec((A_TQ, MIX_WIDTH), lambda b, n: (b * nq + n, 0))] + r_out,
        out_shape=[jax.ShapeDtypeStruct((TOKENS, MIX_WIDTH), BF16)] + r_shape,
        compiler_params=_params("arbitrary", "arbitrary"),
        name="attn_window",
    )(sink, proj, proj, proj, proj, proj, proj, proj, proj, *tail_w)


NB_KEYS = NB_WIN_H * GRID_W


def _rpb_table_kernel(rpb_ref, out_ref):
    h = pl.program_id(0)
    wq = lax.broadcasted_iota(jnp.int32, (GRID_W, GRID_W), 0)
    wk = lax.broadcasted_iota(jnp.int32, (GRID_W, GRID_W), 1)
    dx = jnp.clip(wk - wq + NB_WIN_W - 1, 0, NB_DX - 1)
    cs = jnp.clip(wq - NB_WIN_W // 2, 0, GRID_W - NB_WIN_W)
    col_valid = (wk >= cs) & (wk < cs + NB_WIN_W)
    base = h * (NB_DY * NB_DX)
    tiles = []
    for dy in range(NB_DY):
        acc = jnp.zeros((GRID_W, GRID_W), F32)
        for d in range(NB_DX):
            acc = jnp.where(dx == d, rpb_ref[base + dy * NB_DX + d], acc)
        tiles.append(jnp.where(col_valid, acc * LOG2E, NEG_INF))
    for v in range(NB_WIN_H):
        out_ref[0, v] = jnp.concatenate([tiles[v + y] for y in range(NB_WIN_H)], axis=-1)


def _rpb_table(rpb_all, layer_j):
    per_layer = B_HEADS * NB_DY * NB_DX
    return pl.pallas_call(
        _rpb_table_kernel,
        grid=(B_HEADS,),
        in_specs=[pl.BlockSpec(memory_space=pltpu.SMEM)],
        out_specs=pl.BlockSpec((1, NB_WIN_H, GRID_W, NB_KEYS), lambda h: (h, 0, 0, 0)),
        out_shape=jax.ShapeDtypeStruct((B_HEADS, NB_WIN_H, GRID_W, NB_KEYS), F32),
        compiler_params=_params("arbitrary"),
        name="rpb_table",
    )(rpb_all.reshape(-1)[layer_j * per_layer:(layer_j + 1) * per_layer])


NB_GROUP = 4


def _attn_b_kernel(tbl_ref, q_ref, k_ref, v_ref, g_ref, wo_f, wg_f, wp_f, o_ref, wo_b, wg_b, wp_b):
    _rider_body(pl.program_id(0) * BATCH + pl.program_id(1), wo_f, wg_f, wp_f, wo_b, wg_b, wp_b)

    def window(r):
        rs = min(max(r - NB_WIN_H // 2, 0), GRID_ROWS - NB_WIN_H)
        return rs, rs - r + NB_WIN_H - 1

    def scores(r):
        rs, dy0 = window(r)
        q = q_ref[r * GRID_W:(r + 1) * GRID_W, :]
        k = k_ref[rs * GRID_W:rs * GRID_W + NB_KEYS, :]
        return _dot_nt(q, k) + tbl_ref[0, dy0]

    def finish(r, s):
        rs, _ = window(r)
        rows = slice(r * GRID_W, (r + 1) * GRID_W)
        m = jnp.max(s, axis=-1, keepdims=True)
        p = jnp.exp2(s - m)
        l = jnp.sum(p, axis=-1, keepdims=True)
        v = v_ref[rs * GRID_W:rs * GRID_W + NB_KEYS, :]
        o = jnp.dot(p.astype(BF16), v, preferred_element_type=F32) / l
        o_ref[rows, :] = (o * _silu(g_ref[rows, :].astype(F32))).astype(BF16)

    groups = [range(r, r + NB_GROUP) for r in range(0, GRID_ROWS, NB_GROUP)]
    pending = [scores(r) for r in groups[0]]
    for gi, grp in enumerate(groups):
        current = pending
        if gi + 1 < len(groups):
            pending = [scores(r) for r in groups[gi + 1]]
        for r, s in zip(grp, current):
            finish(r, s)


def _attn_b(proj, tbl, tail_w, layer):
    r_in, r_out, r_shape = _rider_specs(layer, B_HEADS * BATCH, lambda h, b: h * BATCH + b)

    def head(off):
        return pl.BlockSpec((SEQ, HEAD_DIM), lambda h, b: (b, off + h))

    return pl.pallas_call(
        _attn_b_kernel,
        grid=(B_HEADS, BATCH),
        in_specs=[pl.BlockSpec((1, NB_WIN_H, GRID_W, NB_KEYS), lambda h, b: (h, 0, 0, 0)),
                  head(0), head(B_HEADS), head(2 * B_HEADS), head(3 * B_HEADS)] + r_in,
        out_specs=[head(0)] + r_out,
        out_shape=[jax.ShapeDtypeStruct((TOKENS, MIX_WIDTH), BF16)] + r_shape,
        compiler_params=_params("arbitrary", "arbitrary"),
        name="attn_neighbourhood",
    )(tbl, proj, proj, proj, proj, *tail_w)


C_TQ = 256
C_NT = SEQ // C_TQ
C_ND = 2 * C_NT - 1


def _alibi_table_kernel(slope_ref, out_ref):
    h = pl.program_id(0)
    ii = lax.broadcasted_iota(jnp.int32, (C_TQ, C_TQ), 0)
    jj = lax.broadcasted_iota(jnp.int32, (C_TQ, C_TQ), 1)
    for d in range(C_ND):
        dist = jnp.abs((C_NT - 1 - d) * C_TQ + ii - jj).astype(F32)
        out_ref[0, d] = (-slope_ref[h] * LOG2E) * dist


def _alibi_table():
    slopes = jnp.asarray(np.array(_alibi_slopes(C_HEADS), dtype=np.float32))
    return pl.pallas_call(
        _alibi_table_kernel,
        grid=(C_HEADS,),
        in_specs=[pl.BlockSpec(memory_space=pltpu.SMEM)],
        out_specs=pl.BlockSpec((1, C_ND, C_TQ, C_TQ), lambda h: (h, 0, 0, 0)),
        out_shape=jax.ShapeDtypeStruct((C_HEADS, C_ND, C_TQ, C_TQ), F32),
        compiler_params=_params("arbitrary"),
        name="alibi_table",
    )(slopes)


def _attn_c_kernel(lam_ref, subln_ref, tbl_ref, q_ref, k_ref, v_ref, g_ref, wo_f, wg_f, wp_f,
                   o_ref, wo_b, wg_b, wp_b, s_scr, pd_scr, *, lambda_init):
    _rider_body(pl.program_id(0) * C_HEADS + pl.program_id(1), wo_f, wg_f, wp_f, wo_b, wg_b, wp_b)
    lp = lam_ref[...]
    la = jnp.sum(lp[0:1] * lp[1:2], axis=-1, keepdims=True)
    lb = jnp.sum(lp[2:3] * lp[3:4], axis=-1, keepdims=True)
    lam = jnp.exp(la) - jnp.exp(lb) + lambda_init
    maps = [slice(mp * HEAD_DIM, (mp + 1) * HEAD_DIM) for mp in range(2)]
    tiles = [slice(c * C_TQ, (c + 1) * C_TQ) for c in range(C_NT)]

    def scores(n, slot):
        rows = tiles[n]
        s = [_dot_nt(q_ref[rows, mc], k_ref[:, mc]) for mc in maps]
        mx = [None, None]
        for c, tc in enumerate(tiles):
            bias = tbl_ref[0, c - n + C_NT - 1]
            for mp in range(2):
                x = s[mp][:, tc] + bias
                s_scr[slot, mp, :, tc] = x
                mx[mp] = x if mx[mp] is None else jnp.maximum(mx[mp], x)
        return [jnp.max(x, axis=-1, keepdims=True) for x in mx]

    def finish(n, slot, m):
        rows = tiles[n]
        acc = [None, None]
        for tc in tiles:
            for mp in range(2):
                p = jnp.exp2(s_scr[slot, mp, :, tc] - m[mp])
                s_scr[slot, mp, :, tc] = p
                acc[mp] = p if acc[mp] is None else acc[mp] + p
        l = [jnp.sum(x, axis=-1, keepdims=True) for x in acc]
        w0 = 1.0 / l[0]
        w1 = lam / l[1]
        for tc in tiles:
            pd_scr[slot, :, tc] = (s_scr[slot, 0, :, tc] * w0 - s_scr[slot, 1, :, tc] * w1).astype(BF16)
        o = jnp.dot(pd_scr[slot], v_ref[...], preferred_element_type=F32)
        o = _rms(o, subln_ref[...]) * (1.0 - lambda_init)
        o_ref[rows, :] = (o * _silu(g_ref[rows, :].astype(F32))).astype(BF16)

    m_next = scores(0, 0)
    for n in range(C_NT):
        m_cur = m_next
        if n + 1 < C_NT:
            m_next = scores(n + 1, (n + 1) % 2)
        finish(n, n % 2, m_cur)


def _attn_c(proj, tbl, lam_all, subln_all, layer_j, lambda_init, tail_w, layer):
    r_in, r_out, r_shape = _rider_specs(layer, BATCH * C_HEADS, lambda b, h: b * C_HEADS + h)

    def head(off):
        return pl.BlockSpec((SEQ, C_V_DIM), lambda b, h: (b, off + h))

    return pl.pallas_call(
        functools.partial(_attn_c_kernel, lambda_init=lambda_init),
        grid=(BATCH, C_HEADS),
        in_specs=[pl.BlockSpec((None, 4, HEAD_DIM), lambda b, h: (layer_j, 0, 0)),
                  pl.BlockSpec((None, 1, C_V_DIM), lambda b, h: (layer_j, 0, 0)),
                  pl.BlockSpec((1, C_ND, C_TQ, C_TQ), lambda b, h: (h, 0, 0, 0)),
                  head(0), head(C_HEADS), head(2 * C_HEADS), head(3 * C_HEADS)] + r_in,
        out_specs=[head(0)] + r_out,
        out_shape=[jax.ShapeDtypeStruct((TOKENS, MIX_WIDTH), BF16)] + r_shape,
        scratch_shapes=[pltpu.VMEM((2, 2, C_TQ, SEQ), F32), pltpu.VMEM((2, C_TQ, SEQ), BF16)],
        compiler_params=_params("arbitrary", "arbitrary"),
        name="attn_diff",
    )(lam_all, subln_all.reshape(-1, 1, C_V_DIM), tbl, proj, proj, proj, proj, *tail_w)


TAIL_TM = 256
TAIL_SUB = 128


def _tail_kernel(*refs, with_h):
    og_ref, x_ref, p_ref, wo_ref, wg_ref, wp_ref, npost_ref = refs[:7]
    if with_h:
        npre_ref, xo_ref, h_ref = refs[7:]
    else:
        xo_ref, = refs[7:]
    subs = [slice(a, a + TAIL_SUB) for a in range(0, TAIL_TM, TAIL_SUB)]

    def out_proj(r):
        return jnp.dot(og_ref[r, :], wo_ref[...], preferred_element_type=F32)

    def gate_embed(r, y):
        x1 = x_ref[r, :] + _rms(y, npost_ref[...])
        pe = jnp.dot(p_ref[r, :].astype(BF16), wp_ref[...], preferred_element_type=F32)
        gate = jnp.dot(x1.astype(BF16), wg_ref[...], preferred_element_type=F32)
        return x1, gate, pe

    def finish(r, x1, gate, pe):
        x2 = x1 + jax.nn.sigmoid(gate) * pe
        xo_ref[r, :] = x2
        if with_h:
            h_ref[r, :] = _rms(x2, npre_ref[...]).astype(BF16)

    n = len(subs)
    ys, mids = {0: out_proj(subs[0])}, {}
    for t in range(n + 1):
        if t + 1 < n:
            ys[t + 1] = out_proj(subs[t + 1])
        if t < n:
            mids[t] = gate_embed(subs[t], ys.pop(t))
        if t >= 1:
            finish(subs[t - 1], *mids.pop(t - 1))


def _tail(og, x2d, p3d, wo, wg, wp, norm_post3, norm_pre3, layer):
    tm = TAIL_TM
    with_h = layer + 1 < DEPTH
    row = lambda w: pl.BlockSpec((tm, w), lambda i: (i, 0))
    const = lambda a, b, l: pl.BlockSpec((None, a, b), lambda i: (l, 0, 0), pipeline_mode=pl.Buffered(1))
    weight = lambda a, b: pl.BlockSpec((a, b), lambda i: (0, 0), pipeline_mode=pl.Buffered(1))
    in_specs = [row(MIX_WIDTH), row(D_MODEL),
                pl.BlockSpec((None, tm, PE_DIM), lambda i: (layer, i, 0)),
                weight(MIX_WIDTH, D_MODEL), weight(D_MODEL, D_MODEL), weight(PE_DIM, D_MODEL),
                const(1, D_MODEL, layer)]
    args = [og, x2d, p3d, wo, wg, wp, norm_post3]
    out_specs = [row(D_MODEL)]
    out_shape = [jax.ShapeDtypeStruct((TOKENS, D_MODEL), F32)]
    if with_h:
        in_specs.append(const(1, D_MODEL, layer + 1))
        args.append(norm_pre3)
        out_specs.append(row(D_MODEL))
        out_shape.append(jax.ShapeDtypeStruct((TOKENS, D_MODEL), BF16))
    res = pl.pallas_call(
        functools.partial(_tail_kernel, with_h=with_h),
        grid=(TOKENS // tm,),
        in_specs=in_specs,
        out_specs=out_specs,
        out_shape=out_shape,
        compiler_params=_params("arbitrary"),
        name="tail",
    )(*args)
    return (res[0], res[1]) if with_h else (res[0], None)


def kernel(x, p, norm_pre, norm_post, w_out, pe_proj, pe_gate, a_w_in, a_sink, b_w_in, b_rpb,
           c_w_in, c_lambda, c_subln):
    x2d = x.reshape(TOKENS, D_MODEL)
    p3d = p.reshape(DEPTH, TOKENS, PE_DIM)
    norm_pre3 = norm_pre.reshape(DEPTH, 1, D_MODEL)
    norm_post3 = norm_post.reshape(DEPTH, 1, D_MODEL)
    tail_w = (w_out, pe_gate, pe_proj)
    a_tiles = A_IN // PROJ_TN
    g_tiles = MIX_WIDTH // PROJ_TN
    h = _prenorm(x2d, norm_pre3)
    for i in range(DEPTH):
        kind, j = i % N_MIXERS, i // N_MIXERS
        if kind == 0:
            proj = _in_proj(h, a_w_in, j, _colscale(A_IN, A_Q), lambda t: (t + g_tiles) % a_tiles)
            og, wo, wg, wp = _attn_a(proj, a_sink[j], tail_w, i)
        elif kind == 1:
            proj = _in_proj(h, b_w_in, j, _colscale(B_IN, MIX_WIDTH), lambda t: t)
            og, wo, wg, wp = _attn_b(proj, _rpb_table(b_rpb, j), tail_w, i)
        else:
            lambda_init = 0.8 - 0.6 * math.exp(-0.3 * i)
            proj = _in_proj(h, c_w_in, j, _colscale(C_IN, MIX_WIDTH), lambda t: t)
            og, wo, wg, wp = _attn_c(proj, _alibi_table(), c_lambda, c_subln, j, lambda_init, tail_w, i)
        x2d, h = _tail(og, x2d, p3d, wo, wg, wp, norm_post3, norm_pre3, i)
    return x2d.reshape(BATCH, SEQ, D_MODEL)
```

```python
import functools
import math

import jax
import jax.numpy as jnp
import numpy as np
from jax import lax
from jax.experimental import pallas as pl
from jax.experimental.pallas import tpu as pltpu

F32 = jnp.float32
BF16 = jnp.bfloat16

D_MODEL = 2048
BATCH = 4
SEQ = 2048
DEPTH = 4
TOKENS = BATCH * SEQ
PE_DIM = 256
GRID_W = 64
GRID_ROWS = SEQ // GRID_W
N_MIXERS = 3
HEAD_DIM = 128
MIX_WIDTH = D_MODEL
EPS = 1e-6
NEG_INF = -1e30
LOG2E = math.log2(math.e)
Q_SCALE = HEAD_DIM ** -0.5 * LOG2E

A_HEADS = 16
A_KV_HEADS = 4
A_GROUP = A_HEADS // A_KV_HEADS
A_WINDOW = 128
A_Q = A_HEADS * HEAD_DIM
A_KV = A_KV_HEADS * HEAD_DIM
A_IN = A_Q + 2 * A_KV + MIX_WIDTH

B_HEADS = 16
NB_WIN_H = 8
NB_WIN_W = 16
NB_DY = 2 * NB_WIN_H - 1
NB_DX = 2 * NB_WIN_W - 1
B_IN = 3 * B_HEADS * HEAD_DIM + MIX_WIDTH

C_HEADS = 8
C_V_DIM = 2 * HEAD_DIM
C_IN = 4 * MIX_WIDTH

VMEM_LIMIT_BYTES = 52 * 1024 * 1024


def _alibi_slopes(n_heads):
    return [2.0 ** (-8.0 * (h + 1) / n_heads) for h in range(n_heads)]


def _params(*sem, vmem=VMEM_LIMIT_BYTES):
    return pltpu.CompilerParams(dimension_semantics=sem, vmem_limit_bytes=vmem)


def _rms(xf, w):
    ms = jnp.mean(xf * xf, axis=-1, keepdims=True)
    return xf * lax.rsqrt(ms + EPS) * w


def _silu(g):
    return g * jax.nn.sigmoid(g)


def _dot_nt(a, b):
    return lax.dot_general(a, b, (((1,), (1,)), ((), ())), preferred_element_type=F32)


def _prenorm_kernel(x_ref, w_ref, h_ref):
    h_ref[...] = _rms(x_ref[...], w_ref[...]).astype(BF16)


def _prenorm(x2d, norm_pre3):
    tm = 512
    return pl.pallas_call(
        _prenorm_kernel,
        grid=(TOKENS // tm,),
        in_specs=[pl.BlockSpec((tm, D_MODEL), lambda i: (i, 0)),
                  pl.BlockSpec((None, 1, D_MODEL), lambda i: (0, 0, 0))],
        out_specs=pl.BlockSpec((tm, D_MODEL), lambda i: (i, 0)),
        out_shape=jax.ShapeDtypeStruct((TOKENS, D_MODEL), BF16),
        compiler_params=_params("arbitrary"),
        name="prenorm",
    )(x2d, norm_pre3)


PROJ_TM = 1024
PROJ_TN = 1024


def _in_proj_kernel(h_ref, w_ref, cs_ref, o_ref, wb_ref):
    @pl.when(pl.program_id(1) == 0)
    def _():
        wb_ref[...] = w_ref[...].astype(BF16)

    acc = jnp.dot(h_ref[...], wb_ref[...], preferred_element_type=F32)
    o_ref[...] = (acc * cs_ref[...]).astype(o_ref.dtype)


def _in_proj(h, w_all, layer_j, colscale, out_tile):
    n = w_all.shape[2]
    return pl.pallas_call(
        _in_proj_kernel,
        grid=(n // PROJ_TN, TOKENS // PROJ_TM),
        in_specs=[pl.BlockSpec((PROJ_TM, D_MODEL), lambda j, i: (i, 0)),
                  pl.BlockSpec((None, D_MODEL, PROJ_TN), lambda j, i: (layer_j, 0, j)),
                  pl.BlockSpec((1, PROJ_TN), lambda j, i: (0, j))],
        out_specs=pl.BlockSpec((PROJ_TM, PROJ_TN), lambda j, i: (i, out_tile(j))),
        out_shape=jax.ShapeDtypeStruct((TOKENS, n), BF16),
        scratch_shapes=[pltpu.VMEM((D_MODEL, PROJ_TN), BF16)],
        compiler_params=_params("arbitrary", "arbitrary"),
        name="in_proj",
    )(h, w_all, colscale)


def _colscale(n, n_q):
    cs = np.ones((1, n), np.float32)
    cs[:, :n_q] = Q_SCALE
    return jnp.asarray(cs)


def _rider_specs(layer, nsteps, step_of):
    rows = D_MODEL // nsteps
    slab_in = pl.BlockSpec((None, rows, D_MODEL), lambda *g: (layer, step_of(*g), 0))
    slab_out = pl.BlockSpec((rows, D_MODEL), lambda *g: (step_of(*g), 0))
    in_specs = [slab_in, slab_in, pl.BlockSpec((None, PE_DIM, D_MODEL), lambda *g: (layer, 0, 0))]
    out_specs = [slab_out, slab_out, pl.BlockSpec((PE_DIM, D_MODEL), lambda *g: (0, 0))]
    out_shape = [jax.ShapeDtypeStruct((D_MODEL, D_MODEL), BF16), jax.ShapeDtypeStruct((D_MODEL, D_MODEL), BF16),
                 jax.ShapeDtypeStruct((PE_DIM, D_MODEL), BF16)]
    return in_specs, out_specs, out_shape


def _rider_body(step, wo_f, wg_f, wp_f, wo_b, wg_b, wp_b):
    @pl.when(step == 0)
    def _():
        wp_b[...] = wp_f[...].astype(BF16)

    wo_b[...] = wo_f[...].astype(BF16)
    wg_b[...] = wg_f[...].astype(BF16)


A_TQ = 256
A_KEYS = 3 * A_WINDOW


def _attn_a_kernel(sink_ref, q_ref, kp_ref, km_ref, kn_ref, vp_ref, vm_ref, vn_ref, g_ref, wo_f, wg_f, wp_f,
                   o_ref, wo_b, wg_b, wp_b):
    n = pl.program_id(1)
    _rider_body(pl.program_id(0) * (SEQ // A_TQ) + n, wo_f, wg_f, wp_f, wo_b, wg_b, wp_b)
    nsub = A_TQ // A_WINDOW
    qi = lax.broadcasted_iota(jnp.int32, (A_WINDOW, A_WINDOW), 0)
    kr = lax.broadcasted_iota(jnp.int32, (A_WINDOW, A_WINDOW), 1)
    d_prev = (qi + A_WINDOW - kr).astype(F32)
    d_mid = jnp.abs(qi - kr).astype(F32)
    d_next = (kr + A_WINDOW - qi).astype(F32)
    slopes = _alibi_slopes(A_HEADS)
    ones = jnp.ones((A_TQ + 2 * A_WINDOW, HEAD_DIM), BF16)
    kcat, vcat = [], []
    for kv in range(A_KV_HEADS):
        c = slice(kv * HEAD_DIM, (kv + 1) * HEAD_DIM)
        kcat.append(jnp.concatenate([kp_ref[:, c], km_ref[:, c], kn_ref[:, c]], axis=0))
        vrows = jnp.concatenate([vp_ref[:, c], vm_ref[:, c], vn_ref[:, c]], axis=0)
        vcat.append(jnp.concatenate([vrows, ones], axis=1))

    def heads_of(kv):
        return [kv * A_GROUP + g for g in range(A_GROUP)]

    def scores(kv, j):
        rows = slice(j * A_WINDOW, (j + 1) * A_WINDOW)
        qs = jnp.concatenate([q_ref[rows, h * HEAD_DIM:(h + 1) * HEAD_DIM] for h in heads_of(kv)], axis=0)
        return _dot_nt(qs, kcat[kv][j * A_WINDOW:j * A_WINDOW + A_KEYS])

    def finish(kv, j, s):
        rows = slice(j * A_WINDOW, (j + 1) * A_WINDOW)
        off_prev = jnp.where(n > 0, 0, A_WINDOW) if j == 0 else 0
        off_next = jnp.where(n < SEQ // A_TQ - 1, 0, A_WINDOW) if j == nsub - 1 else 0
        ok_prev = kr >= qi + off_prev
        ok_next = kr <= qi - off_next
        probs, sinks = [], []
        for g, h in enumerate(heads_of(kv)):
            sg = s[g * A_WINDOW:(g + 1) * A_WINDOW]
            ch = slopes[h] * LOG2E
            s0 = jnp.where(ok_prev, sg[:, :A_WINDOW] - ch * d_prev, NEG_INF)
            s1 = sg[:, A_WINDOW:2 * A_WINDOW] - ch * d_mid
            s2 = jnp.where(ok_next, sg[:, 2 * A_WINDOW:] - ch * d_next, NEG_INF)
            sink = sink_ref[h] * LOG2E
            m = jnp.max(jnp.maximum(jnp.maximum(s0, s1), s2), axis=-1, keepdims=True)
            m = jnp.maximum(m, sink)
            sinks.append(jnp.exp2(sink - m))
            probs.append(jnp.concatenate([jnp.exp2(s0 - m), jnp.exp2(s1 - m), jnp.exp2(s2 - m)],
                                         axis=-1).astype(BF16))
        ol = jnp.dot(jnp.concatenate(probs, axis=0), vcat[kv][j * A_WINDOW:j * A_WINDOW + A_KEYS],
                     preferred_element_type=F32)
        for g, h in enumerate(heads_of(kv)):
            hc = slice(h * HEAD_DIM, (h + 1) * HEAD_DIM)
            og = ol[g * A_WINDOW:(g + 1) * A_WINDOW]
            o = og[:, :HEAD_DIM] / (og[:, HEAD_DIM:] + sinks[g])
            o_ref[rows, hc] = (o * _silu(g_ref[rows, hc].astype(F32))).astype(BF16)

    units = [(kv, j) for kv in range(A_KV_HEADS) for j in range(nsub)]
    pending = scores(*units[0])
    for u, unit in enumerate(units):
        current = pending
        if u + 1 < len(units):
            pending = scores(*units[u + 1])
        finish(*unit, current)


def _attn_a(proj, sink, tail_w, layer):
    nq = SEQ // A_TQ
    r_in, r_out, r_shape = _rider_specs(layer, BATCH * nq, lambda b, n: b * nq + n)
    r128 = A_TQ // A_WINDOW
    nb128 = SEQ // A_WINDOW
    kcol = (MIX_WIDTH + A_Q) // A_KV
    vcol = kcol + 1

    def halo(col, which):
        if which == "prev":
            return pl.BlockSpec((A_WINDOW, A_KV),
                                lambda b, n: (b * nb128 + jnp.maximum(n * r128 - 1, 0), col))
        if which == "next":
            return pl.BlockSpec((A_WINDOW, A_KV),
                                lambda b, n: (b * nb128 + jnp.minimum((n + 1) * r128, nb128 - 1), col))
        return pl.BlockSpec((A_TQ, A_KV), lambda b, n: (b * nq + n, col))

    return pl.pallas_call(
        _attn_a_kernel,
        grid=(BATCH, nq),
        in_specs=[pl.BlockSpec(memory_space=pltpu.SMEM),
                  pl.BlockSpec((A_TQ, A_Q), lambda b, n: (b * nq + n, 1)),
                  halo(kcol, "prev"), halo(kcol, "main"), halo(kcol, "next"),
                  halo(vcol, "prev"), halo(vcol, "main"), halo(vcol, "next"),
                  pl.BlockSpec((A_TQ, MIX_WIDTH), lambda b, n: (b * nq + n, 0))] + r_in,
        out_specs=[pl.BlockSpec((A_TQ, MIX_WIDTH), lambda b, n: (b * nq + n, 0))] + r_out,
        out_shape=[jax.ShapeDtypeStruct((TOKENS, MIX_WIDTH), BF16)] + r_shape,
        compiler_params=_params("arbitrary", "arbitrary"),
        name="attn_window",
    )(sink, proj, proj, proj, proj, proj, proj, proj, proj, *tail_w)


NB_KEYS = NB_WIN_H * GRID_W


def _rpb_table_kernel(rpb_ref, out_ref):
    h = pl.program_id(0)
    wq = lax.broadcasted_iota(jnp.int32, (GRID_W, GRID_W), 0)
    wk = lax.broadcasted_iota(jnp.int32, (GRID_W, GRID_W), 1)
    dx = jnp.clip(wk - wq + NB_WIN_W - 1, 0, NB_DX - 1)
    cs = jnp.clip(wq - NB_WIN_W // 2, 0, GRID_W - NB_WIN_W)
    col_valid = (wk >= cs) & (wk < cs + NB_WIN_W)
    base = h * (NB_DY * NB_DX)
    tiles = []
    for dy in range(NB_DY):
        acc = jnp.zeros((GRID_W, GRID_W), F32)
        for d in range(NB_DX):
            acc = jnp.where(dx == d, rpb_ref[base + dy * NB_DX + d], acc)
        tiles.append(jnp.where(col_valid, acc * LOG2E, NEG_INF))
    for v in range(NB_WIN_H):
        out_ref[0, v] = jnp.concatenate([tiles[v + y] for y in range(NB_WIN_H)], axis=-1)


def _rpb_table(rpb_all, layer_j):
    per_layer = B_HEADS * NB_DY * NB_DX
    return pl.pallas_call(
        _rpb_table_kernel,
        grid=(B_HEADS,),
        in_specs=[pl.BlockSpec(memory_space=pltpu.SMEM)],
        out_specs=pl.BlockSpec((1, NB_WIN_H, GRID_W, NB_KEYS), lambda h: (h, 0, 0, 0)),
        out_shape=jax.ShapeDtypeStruct((B_HEADS, NB_WIN_H, GRID_W, NB_KEYS), F32),
        compiler_params=_params("arbitrary"),
        name="rpb_table",
    )(rpb_all.reshape(-1)[layer_j * per_layer:(layer_j + 1) * per_layer])


NB_GROUP = 4
NB_HPS = 2


def _attn_b_kernel(tbl_ref, q_ref, k_ref, v_ref, g_ref, wo_f, wg_f, wp_f, o_ref, wo_b, wg_b, wp_b):
    _rider_body(pl.program_id(0) * BATCH + pl.program_id(1), wo_f, wg_f, wp_f, wo_b, wg_b, wp_b)

    def window(r):
        rs = min(max(r - NB_WIN_H // 2, 0), GRID_ROWS - NB_WIN_H)
        return rs, rs - r + NB_WIN_H - 1

    def scores(hh, r):
        rs, dy0 = window(r)
        hc = slice(hh * HEAD_DIM, (hh + 1) * HEAD_DIM)
        q = q_ref[r * GRID_W:(r + 1) * GRID_W, hc]
        k = k_ref[rs * GRID_W:rs * GRID_W + NB_KEYS, hc]
        return _dot_nt(q, k) + tbl_ref[hh, dy0]

    def finish(hh, r, s):
        rs, _ = window(r)
        rows = slice(r * GRID_W, (r + 1) * GRID_W)
        hc = slice(hh * HEAD_DIM, (hh + 1) * HEAD_DIM)
        m = jnp.max(s, axis=-1, keepdims=True)
        p = jnp.exp2(s - m)
        l = jnp.sum(p, axis=-1, keepdims=True)
        v = v_ref[rs * GRID_W:rs * GRID_W + NB_KEYS, hc]
        o = jnp.dot(p.astype(BF16), v, preferred_element_type=F32) / l
        o_ref[rows, hc] = (o * _silu(g_ref[rows, hc].astype(F32))).astype(BF16)

    units = [(hh, r) for hh in range(NB_HPS) for r in range(GRID_ROWS)]
    groups = [units[i:i + NB_GROUP] for i in range(0, len(units), NB_GROUP)]
    pending = [scores(*u) for u in groups[0]]
    for gi, grp in enumerate(groups):
        current = pending
        if gi + 1 < len(groups):
            pending = [scores(*u) for u in groups[gi + 1]]
        for u, s in zip(grp, current):
            finish(*u, s)


def _attn_b(proj, tbl, tail_w, layer):
    steps = B_HEADS // NB_HPS
    r_in, r_out, r_shape = _rider_specs(layer, steps * BATCH, lambda h, b: h * BATCH + b)
    width = NB_HPS * HEAD_DIM

    def heads(off):
        return pl.BlockSpec((SEQ, width), lambda h, b: (b, off + h))

    return pl.pallas_call(
        _attn_b_kernel,
        grid=(steps, BATCH),
        in_specs=[pl.BlockSpec((NB_HPS, NB_WIN_H, GRID_W, NB_KEYS), lambda h, b: (h, 0, 0, 0)),
                  heads(0), heads(steps), heads(2 * steps), heads(3 * steps)] + r_in,
        out_specs=[heads(0)] + r_out,
        out_shape=[jax.ShapeDtypeStruct((TOKENS, MIX_WIDTH), BF16)] + r_shape,
        compiler_params=_params("arbitrary", "arbitrary"),
        name="attn_neighbourhood",
    )(tbl, proj, proj, proj, proj, *tail_w)


C_TQ = 256
C_NT = SEQ // C_TQ
C_ND = 2 * C_NT - 1


def _alibi_table_kernel(slope_ref, out_ref):
    h = pl.program_id(0)
    ii = lax.broadcasted_iota(jnp.int32, (C_TQ, C_TQ), 0)
    jj = lax.broadcasted_iota(jnp.int32, (C_TQ, C_TQ), 1)
    for d in range(C_ND):
        dist = jnp.abs((C_NT - 1 - d) * C_TQ + ii - jj).astype(F32)
        out_ref[0, d] = (-slope_ref[h] * LOG2E) * dist


def _alibi_table():
    slopes = jnp.asarray(np.array(_alibi_slopes(C_HEADS), dtype=np.float32))
    return pl.pallas_call(
        _alibi_table_kernel,
        grid=(C_HEADS,),
        in_specs=[pl.BlockSpec(memory_space=pltpu.SMEM)],
        out_specs=pl.BlockSpec((1, C_ND, C_TQ, C_TQ), lambda h: (h, 0, 0, 0)),
        out_shape=jax.ShapeDtypeStruct((C_HEADS, C_ND, C_TQ, C_TQ), F32),
        compiler_params=_params("arbitrary"),
        name="alibi_table",
    )(slopes)


def _attn_c_kernel(lam_ref, subln_ref, tbl_ref, q_ref, k_ref, v_ref, g_ref, wo_f, wg_f, wp_f,
                   o_ref, wo_b, wg_b, wp_b, s_scr, pd_scr, *, lambda_init):
    _rider_body(pl.program_id(0) * C_HEADS + pl.program_id(1), wo_f, wg_f, wp_f, wo_b, wg_b, wp_b)
    lp = lam_ref[...]
    la = jnp.sum(lp[0:1] * lp[1:2], axis=-1, keepdims=True)
    lb = jnp.sum(lp[2:3] * lp[3:4], axis=-1, keepdims=True)
    lam = jnp.exp(la) - jnp.exp(lb) + lambda_init
    maps = [slice(mp * HEAD_DIM, (mp + 1) * HEAD_DIM) for mp in range(2)]
    tiles = [slice(c * C_TQ, (c + 1) * C_TQ) for c in range(C_NT)]

    def scores(n, slot):
        rows = tiles[n]
        s = [_dot_nt(q_ref[rows, mc], k_ref[:, mc]) for mc in maps]
        mx = [None, None]
        for c, tc in enumerate(tiles):
            bias = tbl_ref[0, c - n + C_NT - 1]
            for mp in range(2):
                x = s[mp][:, tc] + bias
                s_scr[slot, mp, :, tc] = x
                mx[mp] = x if mx[mp] is None else jnp.maximum(mx[mp], x)
        return [jnp.max(x, axis=-1, keepdims=True) for x in mx]

    def finish(n, slot, m):
        rows = tiles[n]
        acc = [None, None]
        for tc in tiles:
            for mp in range(2):
                p = jnp.exp2(s_scr[slot, mp, :, tc] - m[mp])
                s_scr[slot, mp, :, tc] = p
                acc[mp] = p if acc[mp] is None else acc[mp] + p
        l = [jnp.sum(x, axis=-1, keepdims=True) for x in acc]
        w0 = 1.0 / l[0]
        w1 = lam / l[1]
        for tc in tiles:
            pd_scr[slot, :, tc] = (s_scr[slot, 0, :, tc] * w0 - s_scr[slot, 1, :, tc] * w1).astype(BF16)
        o = jnp.dot(pd_scr[slot], v_ref[...], preferred_element_type=F32)
        o = _rms(o, subln_ref[...]) * (1.0 - lambda_init)
        o_ref[rows, :] = (o * _silu(g_ref[rows, :].astype(F32))).astype(BF16)

    m_next = scores(0, 0)
    for n in range(C_NT):
        m_cur = m_next
        if n + 1 < C_NT:
            m_next = scores(n + 1, (n + 1) % 2)
        finish(n, n % 2, m_cur)


def _attn_c(proj, tbl, lam_all, subln_all, layer_j, lambda_init, tail_w, layer):
    r_in, r_out, r_shape = _rider_specs(layer, BATCH * C_HEADS, lambda b, h: b * C_HEADS + h)

    def head(off):
        return pl.BlockSpec((SEQ, C_V_DIM), lambda b, h: (b, off + h))

    return pl.pallas_call(
        functools.partial(_attn_c_kernel, lambda_init=lambda_init),
        grid=(BATCH, C_HEADS),
        in_specs=[pl.BlockSpec((None, 4, HEAD_DIM), lambda b, h: (layer_j, 0, 0)),
                  pl.BlockSpec((None, 1, C_V_DIM), lambda b, h: (layer_j, 0, 0)),
                  pl.BlockSpec((1, C_ND, C_TQ, C_TQ), lambda b, h: (h, 0, 0, 0)),
                  head(0), head(C_HEADS), head(2 * C_HEADS), head(3 * C_HEADS)] + r_in,
        out_specs=[head(0)] + r_out,
        out_shape=[jax.ShapeDtypeStruct((TOKENS, MIX_WIDTH), BF16)] + r_shape,
        scratch_shapes=[pltpu.VMEM((2, 2, C_TQ, SEQ), F32), pltpu.VMEM((2, C_TQ, SEQ), BF16)],
        compiler_params=_params("arbitrary", "arbitrary"),
        name="attn_diff",
    )(lam_all, subln_all.reshape(-1, 1, C_V_DIM), tbl, proj, proj, proj, proj, *tail_w)


TAIL_TM = 512
TAIL_SUB = 256


def _tail_kernel(*refs, with_h):
    og_ref, x_ref, p_ref, wo_ref, wg_ref, wp_ref, npost_ref = refs[:7]
    if with_h:
        npre_ref, xo_ref, h_ref = refs[7:]
    else:
        xo_ref, = refs[7:]
    subs = [slice(a, a + TAIL_SUB) for a in range(0, TAIL_TM, TAIL_SUB)]

    def out_proj(r):
        return jnp.dot(og_ref[r, :], wo_ref[...], preferred_element_type=F32)

    def gate_embed(r, y):
        x1 = x_ref[r, :] + _rms(y, npost_ref[...])
        pe = jnp.dot(p_ref[r, :].astype(BF16), wp_ref[...], preferred_element_type=F32)
        gate = jnp.dot(x1.astype(BF16), wg_ref[...], preferred_element_type=F32)
        return x1, gate, pe

    def finish(r, x1, gate, pe):
        x2 = x1 + jax.nn.sigmoid(gate) * pe
        xo_ref[r, :] = x2
        if with_h:
            h_ref[r, :] = _rms(x2, npre_ref[...]).astype(BF16)

    n = len(subs)
    ys, mids = {0: out_proj(subs[0])}, {}
    for t in range(n + 1):
        if t + 1 < n:
            ys[t + 1] = out_proj(subs[t + 1])
        if t < n:
            mids[t] = gate_embed(subs[t], ys.pop(t))
        if t >= 1:
            finish(subs[t - 1], *mids.pop(t - 1))


def _tail(og, x2d, p3d, wo, wg, wp, norm_post3, norm_pre3, layer):
    tm = TAIL_TM
    with_h = layer + 1 < DEPTH
    row = lambda w: pl.BlockSpec((tm, w), lambda i: (i, 0))
    const = lambda a, b, l: pl.BlockSpec((None, a, b), lambda i: (l, 0, 0), pipeline_mode=pl.Buffered(1))
    weight = lambda a, b: pl.BlockSpec((a, b), lambda i: (0, 0), pipeline_mode=pl.Buffered(1))
    in_specs = [row(MIX_WIDTH), row(D_MODEL),
                pl.BlockSpec((None, tm, PE_DIM), lambda i: (layer, i, 0)),
                weight(MIX_WIDTH, D_MODEL), weight(D_MODEL, D_MODEL), weight(PE_DIM, D_MODEL),
                const(1, D_MODEL, layer)]
    args = [og, x2d, p3d, wo, wg, wp, norm_post3]
    out_specs = [row(D_MODEL)]
    out_shape = [jax.ShapeDtypeStruct((TOKENS, D_MODEL), F32)]
    if with_h:
        in_specs.append(const(1, D_MODEL, layer + 1))
        args.append(norm_pre3)
        out_specs.append(row(D_MODEL))
        out_shape.append(jax.ShapeDtypeStruct((TOKENS, D_MODEL), BF16))
    res = pl.pallas_call(
        functools.partial(_tail_kernel, with_h=with_h),
        grid=(TOKENS // tm,),
        in_specs=in_specs,
        out_specs=out_specs,
        out_shape=out_shape,
        compiler_params=_params("arbitrary", vmem=60 * 1024 * 1024),
        name="tail",
    )(*args)
    return (res[0], res[1]) if with_h else (res[0], None)


def kernel(x, p, norm_pre, norm_post, w_out, pe_proj, pe_gate, a_w_in, a_sink, b_w_in, b_rpb,
           c_w_in, c_lambda, c_subln):
    x2d = x.reshape(TOKENS, D_MODEL)
    p3d = p.reshape(DEPTH, TOKENS, PE_DIM)
    norm_pre3 = norm_pre.reshape(DEPTH, 1, D_MODEL)
    norm_post3 = norm_post.reshape(DEPTH, 1, D_MODEL)
    tail_w = (w_out, pe_gate, pe_proj)
    a_tiles = A_IN // PROJ_TN
    g_tiles = MIX_WIDTH // PROJ_TN
    h = _prenorm(x2d, norm_pre3)
    for i in range(DEPTH):
        kind, j = i % N_MIXERS, i // N_MIXERS
        if kind == 0:
            proj = _in_proj(h, a_w_in, j, _colscale(A_IN, A_Q), lambda t: (t + g_tiles) % a_tiles)
            og, wo, wg, wp = _attn_a(proj, a_sink[j], tail_w, i)
        elif kind == 1:
            proj = _in_proj(h, b_w_in, j, _colscale(B_IN, MIX_WIDTH), lambda t: t)
            og, wo, wg, wp = _attn_b(proj, _rpb_table(b_rpb, j), tail_w, i)
        else:
            lambda_init = 0.8 - 0.6 * math.exp(-0.3 * i)
            proj = _in_proj(h, c_w_in, j, _colscale(C_IN, MIX_WIDTH), lambda t: t)
            og, wo, wg, wp = _attn_c(proj, _alibi_table(), c_lambda, c_subln, j, lambda_init, tail_w, i)
        x2d, h = _tail(og, x2d, p3d, wo, wg, wp, norm_post3, norm_pre3, i)
    return x2d.reshape(BATCH, SEQ, D_MODEL)
```

```python
import functools
import math

import jax
import jax.numpy as jnp
import numpy as np
from jax import lax
from jax.experimental import pallas as pl
from jax.experimental.pallas import tpu as pltpu

F32 = jnp.float32
BF16 = jnp.bfloat16

D_MODEL = 2048
BATCH = 4
SEQ = 2048
DEPTH = 4
TOKENS = BATCH * SEQ
PE_DIM = 256
GRID_W = 64
GRID_ROWS = SEQ // GRID_W
N_MIXERS = 3
HEAD_DIM = 128
MIX_WIDTH = D_MODEL
EPS = 1e-6
NEG_INF = -1e30
LOG2E = math.log2(math.e)
Q_SCALE = HEAD_DIM ** -0.5 * LOG2E

A_HEADS = 16
A_KV_HEADS = 4
A_GROUP = A_HEADS // A_KV_HEADS
A_WINDOW = 128
A_Q = A_HEADS * HEAD_DIM
A_KV = A_KV_HEADS * HEAD_DIM
A_IN = A_Q + 2 * A_KV + MIX_WIDTH

B_HEADS = 16
NB_WIN_H = 8
NB_WIN_W = 16
NB_DY = 2 * NB_WIN_H - 1
NB_DX = 2 * NB_WIN_W - 1
B_IN = 3 * B_HEADS * HEAD_DIM + MIX_WIDTH

C_HEADS = 8
C_V_DIM = 2 * HEAD_DIM
C_IN = 4 * MIX_WIDTH

VMEM_LIMIT_BYTES = 52 * 1024 * 1024


def _alibi_slopes(n_heads):
    return [2.0 ** (-8.0 * (h + 1) / n_heads) for h in range(n_heads)]


def _params(*sem, vmem=VMEM_LIMIT_BYTES):
    return pltpu.CompilerParams(dimension_semantics=sem, vmem_limit_bytes=vmem)


def _rms(xf, w):
    ms = jnp.mean(xf * xf, axis=-1, keepdims=True)
    return xf * lax.rsqrt(ms + EPS) * w


def _silu(g):
    return g * jax.nn.sigmoid(g)


def _dot_nt(a, b):
    return lax.dot_general(a, b, (((1,), (1,)), ((), ())), preferred_element_type=F32)


def _prenorm_kernel(x_ref, w_ref, h_ref):
    h_ref[...] = _rms(x_ref[...], w_ref[...]).astype(BF16)


def _prenorm(x2d, norm_pre3):
    tm = 512
    return pl.pallas_call(
        _prenorm_kernel,
        grid=(TOKENS // tm,),
        in_specs=[pl.BlockSpec((tm, D_MODEL), lambda i: (i, 0)),
                  pl.BlockSpec((None, 1, D_MODEL), lambda i: (0, 0, 0))],
        out_specs=pl.BlockSpec((tm, D_MODEL), lambda i: (i, 0)),
        out_shape=jax.ShapeDtypeStruct((TOKENS, D_MODEL), BF16),
        compiler_params=_params("arbitrary"),
        name="prenorm",
    )(x2d, norm_pre3)


PROJ_TM = 1024
PROJ_TN = 1024


def _in_proj_kernel(h_ref, w_ref, cs_ref, o_ref, wb_ref):
    @pl.when(pl.program_id(1) == 0)
    def _():
        wb_ref[...] = w_ref[...].astype(BF16)

    acc = jnp.dot(h_ref[...], wb_ref[...], preferred_element_type=F32)
    o_ref[...] = (acc * cs_ref[...]).astype(o_ref.dtype)


def _in_proj(h, w_all, layer_j, colscale, out_tile):
    n = w_all.shape[2]
    return pl.pallas_call(
        _in_proj_kernel,
        grid=(n // PROJ_TN, TOKENS // PROJ_TM),
        in_specs=[pl.BlockSpec((PROJ_TM, D_MODEL), lambda j, i: (i, 0)),
                  pl.BlockSpec((None, D_MODEL, PROJ_TN), lambda j, i: (layer_j, 0, j)),
                  pl.BlockSpec((1, PROJ_TN), lambda j, i: (0, j))],
        out_specs=pl.BlockSpec((PROJ_TM, PROJ_TN), lambda j, i: (i, out_tile(j))),
        out_shape=jax.ShapeDtypeStruct((TOKENS, n), BF16),
        scratch_shapes=[pltpu.VMEM((D_MODEL, PROJ_TN), BF16)],
        compiler_params=_params("arbitrary", "arbitrary"),
        name="in_proj",
    )(h, w_all, colscale)


def _colscale(n, n_q):
    cs = np.ones((1, n), np.float32)
    cs[:, :n_q] = Q_SCALE
    return jnp.asarray(cs)


def _rider_specs(layer, nsteps, step_of):
    rows = D_MODEL // nsteps
    slab_in = pl.BlockSpec((None, rows, D_MODEL), lambda *g: (layer, step_of(*g), 0))
    slab_out = pl.BlockSpec((rows, D_MODEL), lambda *g: (step_of(*g), 0))
    in_specs = [slab_in, slab_in, pl.BlockSpec((None, PE_DIM, D_MODEL), lambda *g: (layer, 0, 0))]
    out_specs = [slab_out, slab_out, pl.BlockSpec((PE_DIM, D_MODEL), lambda *g: (0, 0))]
    out_shape = [jax.ShapeDtypeStruct((D_MODEL, D_MODEL), BF16), jax.ShapeDtypeStruct((D_MODEL, D_MODEL), BF16),
                 jax.ShapeDtypeStruct((PE_DIM, D_MODEL), BF16)]
    return in_specs, out_specs, out_shape


def _rider_body(step, wo_f, wg_f, wp_f, wo_b, wg_b, wp_b):
    @pl.when(step == 0)
    def _():
        wp_b[...] = wp_f[...].astype(BF16)

    wo_b[...] = wo_f[...].astype(BF16)
    wg_b[...] = wg_f[...].astype(BF16)


A_TQ = 256
A_KEYS = 3 * A_WINDOW


def _attn_a_kernel(sink_ref, q_ref, kp_ref, km_ref, kn_ref, vp_ref, vm_ref, vn_ref, g_ref, wo_f, wg_f, wp_f,
                   o_ref, wo_b, wg_b, wp_b):
    n = pl.program_id(1)
    _rider_body(pl.program_id(0) * (SEQ // A_TQ) + n, wo_f, wg_f, wp_f, wo_b, wg_b, wp_b)
    nsub = A_TQ // A_WINDOW
    qi = lax.broadcasted_iota(jnp.int32, (A_WINDOW, A_WINDOW), 0)
    kr = lax.broadcasted_iota(jnp.int32, (A_WINDOW, A_WINDOW), 1)
    d_prev = (qi + A_WINDOW - kr).astype(F32)
    d_mid = jnp.abs(qi - kr).astype(F32)
    d_next = (kr + A_WINDOW - qi).astype(F32)
    slopes = _alibi_slopes(A_HEADS)
    ones = jnp.ones((A_TQ + 2 * A_WINDOW, HEAD_DIM), BF16)
    kcat, vcat = [], []
    for kv in range(A_KV_HEADS):
        c = slice(kv * HEAD_DIM, (kv + 1) * HEAD_DIM)
        kcat.append(jnp.concatenate([kp_ref[:, c], km_ref[:, c], kn_ref[:, c]], axis=0))
        vrows = jnp.concatenate([vp_ref[:, c], vm_ref[:, c], vn_ref[:, c]], axis=0)
        vcat.append(jnp.concatenate([vrows, ones], axis=1))

    def heads_of(kv):
        return [kv * A_GROUP + g for g in range(A_GROUP)]

    def scores(kv, j):
        rows = slice(j * A_WINDOW, (j + 1) * A_WINDOW)
        qs = jnp.concatenate([q_ref[rows, h * HEAD_DIM:(h + 1) * HEAD_DIM] for h in heads_of(kv)], axis=0)
        return _dot_nt(qs, kcat[kv][j * A_WINDOW:j * A_WINDOW + A_KEYS])

    def finish(kv, j, s):
        rows = slice(j * A_WINDOW, (j + 1) * A_WINDOW)
        off_prev = jnp.where(n > 0, 0, A_WINDOW) if j == 0 else 0
        off_next = jnp.where(n < SEQ // A_TQ - 1, 0, A_WINDOW) if j == nsub - 1 else 0
        ok_prev = kr >= qi + off_prev
        ok_next = kr <= qi - off_next
        probs, sinks = [], []
        for g, h in enumerate(heads_of(kv)):
            sg = s[g * A_WINDOW:(g + 1) * A_WINDOW]
            ch = slopes[h] * LOG2E
            s0 = jnp.where(ok_prev, sg[:, :A_WINDOW] - ch * d_prev, NEG_INF)
            s1 = sg[:, A_WINDOW:2 * A_WINDOW] - ch * d_mid
            s2 = jnp.where(ok_next, sg[:, 2 * A_WINDOW:] - ch * d_next, NEG_INF)
            sink = sink_ref[h] * LOG2E
            m = jnp.max(jnp.maximum(jnp.maximum(s0, s1), s2), axis=-1, keepdims=True)
            m = jnp.maximum(m, sink)
            sinks.append(jnp.exp2(sink - m))
            probs.append(jnp.concatenate([jnp.exp2(s0 - m), jnp.exp2(s1 - m), jnp.exp2(s2 - m)],
                                         axis=-1).astype(BF16))
        ol = jnp.dot(jnp.concatenate(probs, axis=0), vcat[kv][j * A_WINDOW:j * A_WINDOW + A_KEYS],
                     preferred_element_type=F32)
        for g, h in enumerate(heads_of(kv)):
            hc = slice(h * HEAD_DIM, (h + 1) * HEAD_DIM)
            og = ol[g * A_WINDOW:(g + 1) * A_WINDOW]
            o = og[:, :HEAD_DIM] / (og[:, HEAD_DIM:] + sinks[g])
            o_ref[rows, hc] = (o * _silu(g_ref[rows, hc].astype(F32))).astype(BF16)

    units = [(kv, j) for kv in range(A_KV_HEADS) for j in range(nsub)]
    pending = scores(*units[0])
    for u, unit in enumerate(units):
        current = pending
        if u + 1 < len(units):
            pending = scores(*units[u + 1])
        finish(*unit, current)


def _attn_a(proj, sink, tail_w, layer):
    nq = SEQ // A_TQ
    r_in, r_out, r_shape = _rider_specs(layer, BATCH * nq, lambda b, n: b * nq + n)
    r128 = A_TQ // A_WINDOW
    nb128 = SEQ // A_WINDOW
    kcol = (MIX_WIDTH + A_Q) // A_KV
    vcol = kcol + 1

    def halo(col, which):
        if which == "prev":
            return pl.BlockSpec((A_WINDOW, A_KV),
                                lambda b, n: (b * nb128 + jnp.maximum(n * r128 - 1, 0), col))
        if which == "next":
            return pl.BlockSpec((A_WINDOW, A_KV),
                                lambda b, n: (b * nb128 + jnp.minimum((n + 1) * r128, nb128 - 1), col))
        return pl.BlockSpec((A_TQ, A_KV), lambda b, n: (b * nq + n, col))

    return pl.pallas_call(
        _attn_a_kernel,
        grid=(BATCH, nq),
        in_specs=[pl.BlockSpec(memory_space=pltpu.SMEM),
                  pl.BlockSpec((A_TQ, A_Q), lambda b, n: (b * nq + n, 1)),
                  halo(kcol, "prev"), halo(kcol, "main"), halo(kcol, "next"),
                  halo(vcol, "prev"), halo(vcol, "main"), halo(vcol, "next"),
                  pl.BlockSpec((A_TQ, MIX_WIDTH), lambda b, n: (b * nq + n, 0))] + r_in,
        out_specs=[pl.BlockSpec((A_TQ, MIX_WIDTH), lambda b, n: (b * nq + n, 0))] + r_out,
        out_shape=[jax.ShapeDtypeStruct((TOKENS, MIX_WIDTH), BF16)] + r_shape,
        compiler_params=_params("arbitrary", "arbitrary"),
        name="attn_window",
    )(sink, proj, proj, proj, proj, proj, proj, proj, proj, *tail_w)


NB_KEYS = NB_WIN_H * GRID_W


def _rpb_table_kernel(rpb_ref, out_ref):
    h = pl.program_id(0)
    wq = lax.broadcasted_iota(jnp.int32, (GRID_W, GRID_W), 0)
    wk = lax.broadcasted_iota(jnp.int32, (GRID_W, GRID_W), 1)
    dx = jnp.clip(wk - wq + NB_WIN_W - 1, 0, NB_DX - 1)
    cs = jnp.clip(wq - NB_WIN_W // 2, 0, GRID_W - NB_WIN_W)
    col_valid = (wk >= cs) & (wk < cs + NB_WIN_W)
    base = h * (NB_DY * NB_DX)
    tiles = []
    for dy in range(NB_DY):
        acc = jnp.zeros((GRID_W, GRID_W), F32)
        for d in range(NB_DX):
            acc = jnp.where(dx == d, rpb_ref[base + dy * NB_DX + d], acc)
        tiles.append(jnp.where(col_valid, acc * LOG2E, NEG_INF))
    for v in range(NB_WIN_H):
        out_ref[0, v] = jnp.concatenate([tiles[v + y] for y in range(NB_WIN_H)], axis=-1)


def _rpb_table(rpb_all, layer_j):
    per_layer = B_HEADS * NB_DY * NB_DX
    return pl.pallas_call(
        _rpb_table_kernel,
        grid=(B_HEADS,),
        in_specs=[pl.BlockSpec(memory_space=pltpu.SMEM)],
        out_specs=pl.BlockSpec((1, NB_WIN_H, GRID_W, NB_KEYS), lambda h: (h, 0, 0, 0)),
        out_shape=jax.ShapeDtypeStruct((B_HEADS, NB_WIN_H, GRID_W, NB_KEYS), F32),
        compiler_params=_params("arbitrary"),
        name="rpb_table",
    )(rpb_all.reshape(-1)[layer_j * per_layer:(layer_j + 1) * per_layer])


NB_GROUP = 4
NB_HPS = 2


def _attn_b_kernel(tbl_ref, q_ref, k_ref, v_ref, g_ref, wo_f, wg_f, wp_f, o_ref, wo_b, wg_b, wp_b, kt_scr):
    _rider_body(pl.program_id(0) * BATCH + pl.program_id(1), wo_f, wg_f, wp_f, wo_b, wg_b, wp_b)
    for hh in range(NB_HPS):
        hc = slice(hh * HEAD_DIM, (hh + 1) * HEAD_DIM)
        kt_scr[hh, 0] = k_ref[:, hc].T
        kt_scr[hh, 1, :, :SEQ - 2 * GRID_W] = k_ref[GRID_W:SEQ - GRID_W, hc].T

    def window(r):
        rs = min(max(r - NB_WIN_H // 2, 0), GRID_ROWS - NB_WIN_H)
        return rs, rs - r + NB_WIN_H - 1

    def scores(hh, r):
        rs, dy0 = window(r)
        hc = slice(hh * HEAD_DIM, (hh + 1) * HEAD_DIM)
        q = q_ref[r * GRID_W:(r + 1) * GRID_W, hc]
        odd = rs % 2
        kt = kt_scr[hh, odd, :, (rs - odd) * GRID_W:(rs - odd) * GRID_W + NB_KEYS]
        return jnp.dot(q, kt, preferred_element_type=F32) + tbl_ref[hh, dy0]

    def finish(hh, r, s):
        rs, _ = window(r)
        rows = slice(r * GRID_W, (r + 1) * GRID_W)
        hc = slice(hh * HEAD_DIM, (hh + 1) * HEAD_DIM)
        m = jnp.max(s, axis=-1, keepdims=True)
        p = jnp.exp2(s - m)
        l = jnp.sum(p, axis=-1, keepdims=True)
        v = v_ref[rs * GRID_W:rs * GRID_W + NB_KEYS, hc]
        o = jnp.dot(p.astype(BF16), v, preferred_element_type=F32) / l
        o_ref[rows, hc] = (o * _silu(g_ref[rows, hc].astype(F32))).astype(BF16)

    units = [(hh, r) for hh in range(NB_HPS) for r in range(GRID_ROWS)]
    groups = [units[i:i + NB_GROUP] for i in range(0, len(units), NB_GROUP)]
    pending = [scores(*u) for u in groups[0]]
    for gi, grp in enumerate(groups):
        current = pending
        if gi + 1 < len(groups):
            pending = [scores(*u) for u in groups[gi + 1]]
        for u, s in zip(grp, current):
            finish(*u, s)


def _attn_b(proj, tbl, tail_w, layer):
    steps = B_HEADS // NB_HPS
    r_in, r_out, r_shape = _rider_specs(layer, steps * BATCH, lambda h, b: h * BATCH + b)
    width = NB_HPS * HEAD_DIM

    def heads(off):
        return pl.BlockSpec((SEQ, width), lambda h, b: (b, off + h))

    return pl.pallas_call(
        _attn_b_kernel,
        grid=(steps, BATCH),
        in_specs=[pl.BlockSpec((NB_HPS, NB_WIN_H, GRID_W, NB_KEYS), lambda h, b: (h, 0, 0, 0)),
                  heads(0), heads(steps), heads(2 * steps), heads(3 * steps)] + r_in,
        out_specs=[heads(0)] + r_out,
        out_shape=[jax.ShapeDtypeStruct((TOKENS, MIX_WIDTH), BF16)] + r_shape,
        scratch_shapes=[pltpu.VMEM((NB_HPS, 2, HEAD_DIM, SEQ), BF16)],
        compiler_params=_params("arbitrary", "arbitrary"),
        name="attn_neighbourhood",
    )(tbl, proj, proj, proj, proj, *tail_w)


C_TQ = 256
C_NT = SEQ // C_TQ
C_ND = 2 * C_NT - 1


def _alibi_table_kernel(slope_ref, out_ref):
    h = pl.program_id(0)
    ii = lax.broadcasted_iota(jnp.int32, (C_TQ, C_TQ), 0)
    jj = lax.broadcasted_iota(jnp.int32, (C_TQ, C_TQ), 1)
    for d in range(C_ND):
        dist = jnp.abs((C_NT - 1 - d) * C_TQ + ii - jj).astype(F32)
        out_ref[0, d] = (-slope_ref[h] * LOG2E) * dist


def _alibi_table():
    slopes = jnp.asarray(np.array(_alibi_slopes(C_HEADS), dtype=np.float32))
    return pl.pallas_call(
        _alibi_table_kernel,
        grid=(C_HEADS,),
        in_specs=[pl.BlockSpec(memory_space=pltpu.SMEM)],
        out_specs=pl.BlockSpec((1, C_ND, C_TQ, C_TQ), lambda h: (h, 0, 0, 0)),
        out_shape=jax.ShapeDtypeStruct((C_HEADS, C_ND, C_TQ, C_TQ), F32),
        compiler_params=_params("arbitrary"),
        name="alibi_table",
    )(slopes)


def _attn_c_kernel(lam_ref, subln_ref, tbl_ref, q_ref, k_ref, v_ref, g_ref, wo_f, wg_f, wp_f,
                   o_ref, wo_b, wg_b, wp_b, s_scr, pd_scr, *, lambda_init):
    _rider_body(pl.program_id(0) * C_HEADS + pl.program_id(1), wo_f, wg_f, wp_f, wo_b, wg_b, wp_b)
    lp = lam_ref[...]
    la = jnp.sum(lp[0:1] * lp[1:2], axis=-1, keepdims=True)
    lb = jnp.sum(lp[2:3] * lp[3:4], axis=-1, keepdims=True)
    lam = jnp.exp(la) - jnp.exp(lb) + lambda_init
    maps = [slice(mp * HEAD_DIM, (mp + 1) * HEAD_DIM) for mp in range(2)]
    tiles = [slice(c * C_TQ, (c + 1) * C_TQ) for c in range(C_NT)]

    def scores(n, slot):
        rows = tiles[n]
        s = [_dot_nt(q_ref[rows, mc], k_ref[:, mc]) for mc in maps]
        mx = [None, None]
        for c, tc in enumerate(tiles):
            bias = tbl_ref[0, c - n + C_NT - 1]
            for mp in range(2):
                x = s[mp][:, tc] + bias
                s_scr[slot, mp, :, tc] = x
                mx[mp] = x if mx[mp] is None else jnp.maximum(mx[mp], x)
        return [jnp.max(x, axis=-1, keepdims=True) for x in mx]

    def finish(n, slot, m):
        rows = tiles[n]
        acc = [None, None]
        for tc in tiles:
            for mp in range(2):
                p = jnp.exp2(s_scr[slot, mp, :, tc] - m[mp])
                s_scr[slot, mp, :, tc] = p
                acc[mp] = p if acc[mp] is None else acc[mp] + p
        l = [jnp.sum(x, axis=-1, keepdims=True) for x in acc]
        w0 = 1.0 / l[0]
        w1 = lam / l[1]
        for tc in tiles:
            pd_scr[slot, :, tc] = (s_scr[slot, 0, :, tc] * w0 - s_scr[slot, 1, :, tc] * w1).astype(BF16)
        o = jnp.dot(pd_scr[slot], v_ref[...], preferred_element_type=F32)
        o = _rms(o, subln_ref[...]) * (1.0 - lambda_init)
        o_ref[rows, :] = (o * _silu(g_ref[rows, :].astype(F32))).astype(BF16)

    m_next = scores(0, 0)
    for n in range(C_NT):
        m_cur = m_next
        if n + 1 < C_NT:
            m_next = scores(n + 1, (n + 1) % 2)
        finish(n, n % 2, m_cur)


def _attn_c(proj, tbl, lam_all, subln_all, layer_j, lambda_init, tail_w, layer):
    r_in, r_out, r_shape = _rider_specs(layer, BATCH * C_HEADS, lambda b, h: b * C_HEADS + h)

    def head(off):
        return pl.BlockSpec((SEQ, C_V_DIM), lambda b, h: (b, off + h))

    return pl.pallas_call(
        functools.partial(_attn_c_kernel, lambda_init=lambda_init),
        grid=(BATCH, C_HEADS),
        in_specs=[pl.BlockSpec((None, 4, HEAD_DIM), lambda b, h: (layer_j, 0, 0)),
                  pl.BlockSpec((None, 1, C_V_DIM), lambda b, h: (layer_j, 0, 0)),
                  pl.BlockSpec((1, C_ND, C_TQ, C_TQ), lambda b, h: (h, 0, 0, 0)),
                  head(0), head(C_HEADS), head(2 * C_HEADS), head(3 * C_HEADS)] + r_in,
        out_specs=[head(0)] + r_out,
        out_shape=[jax.ShapeDtypeStruct((TOKENS, MIX_WIDTH), BF16)] + r_shape,
        scratch_shapes=[pltpu.VMEM((2, 2, C_TQ, SEQ), F32), pltpu.VMEM((2, C_TQ, SEQ), BF16)],
        compiler_params=_params("arbitrary", "arbitrary"),
        name="attn_diff",
    )(lam_all, subln_all.reshape(-1, 1, C_V_DIM), tbl, proj, proj, proj, proj, *tail_w)


TAIL_TM = 512
TAIL_SUB = 256


def _tail_kernel(*refs, with_h):
    og_ref, x_ref, p_ref, wo_ref, wg_ref, wp_ref, npost_ref = refs[:7]
    if with_h:
        npre_ref, xo_ref, h_ref = refs[7:]
    else:
        xo_ref, = refs[7:]
    subs = [slice(a, a + TAIL_SUB) for a in range(0, TAIL_TM, TAIL_SUB)]

    def out_proj(r):
        return jnp.dot(og_ref[r, :], wo_ref[...], preferred_element_type=F32)

    def gate_embed(r, y):
        x1 = x_ref[r, :] + _rms(y, npost_ref[...])
        pe = jnp.dot(p_ref[r, :].astype(BF16), wp_ref[...], preferred_element_type=F32)
        gate = jnp.dot(x1.astype(BF16), wg_ref[...], preferred_element_type=F32)
        return x1, gate, pe

    def finish(r, x1, gate, pe):
        x2 = x1 + jax.nn.sigmoid(gate) * pe
        xo_ref[r, :] = x2
        if with_h:
            h_ref[r, :] = _rms(x2, npre_ref[...]).astype(BF16)

    n = len(subs)
    ys, mids = {0: out_proj(subs[0])}, {}
    for t in range(n + 1):
        if t + 1 < n:
            ys[t + 1] = out_proj(subs[t + 1])
        if t < n:
            mids[t] = gate_embed(subs[t], ys.pop(t))
        if t >= 1:
            finish(subs[t - 1], *mids.pop(t - 1))


def _tail(og, x2d, p3d, wo, wg, wp, norm_post3, norm_pre3, layer):
    tm = TAIL_TM
    with_h = layer + 1 < DEPTH
    row = lambda w: pl.BlockSpec((tm, w), lambda i: (i, 0))
    const = lambda a, b, l: pl.BlockSpec((None, a, b), lambda i: (l, 0, 0), pipeline_mode=pl.Buffered(1))
    weight = lambda a, b: pl.BlockSpec((a, b), lambda i: (0, 0), pipeline_mode=pl.Buffered(1))
    in_specs = [row(MIX_WIDTH), row(D_MODEL),
                pl.BlockSpec((None, tm, PE_DIM), lambda i: (layer, i, 0)),
                weight(MIX_WIDTH, D_MODEL), weight(D_MODEL, D_MODEL), weight(PE_DIM, D_MODEL),
                const(1, D_MODEL, layer)]
    args = [og, x2d, p3d, wo, wg, wp, norm_post3]
    out_specs = [row(D_MODEL)]
    out_shape = [jax.ShapeDtypeStruct((TOKENS, D_MODEL), F32)]
    if with_h:
        in_specs.append(const(1, D_MODEL, layer + 1))
        args.append(norm_pre3)
        out_specs.append(row(D_MODEL))
        out_shape.append(jax.ShapeDtypeStruct((TOKENS, D_MODEL), BF16))
    res = pl.pallas_call(
        functools.partial(_tail_kernel, with_h=with_h),
        grid=(TOKENS // tm,),
        in_specs=in_specs,
        out_specs=out_specs,
        out_shape=out_shape,
        compiler_params=_params("arbitrary", vmem=60 * 1024 * 1024),
        name="tail",
    )(*args)
    return (res[0], res[1]) if with_h else (res[0], None)


def kernel(x, p, norm_pre, norm_post, w_out, pe_proj, pe_gate, a_w_in, a_sink, b_w_in, b_rpb,
           c_w_in, c_lambda, c_subln):
    x2d = x.reshape(TOKENS, D_MODEL)
    p3d = p.reshape(DEPTH, TOKENS, PE_DIM)
    norm_pre3 = norm_pre.reshape(DEPTH, 1, D_MODEL)
    norm_post3 = norm_post.reshape(DEPTH, 1, D_MODEL)
    tail_w = (w_out, pe_gate, pe_proj)
    a_tiles = A_IN // PROJ_TN
    g_tiles = MIX_WIDTH // PROJ_TN
    h = _prenorm(x2d, norm_pre3)
    for i in range(DEPTH):
        kind, j = i % N_MIXERS, i // N_MIXERS
        if kind == 0:
            proj = _in_proj(h, a_w_in, j, _colscale(A_IN, A_Q), lambda t: (t + g_tiles) % a_tiles)
            og, wo, wg, wp = _attn_a(proj, a_sink[j], tail_w, i)
        elif kind == 1:
            proj = _in_proj(h, b_w_in, j, _colscale(B_IN, MIX_WIDTH), lambda t: t)
            og, wo, wg, wp = _attn_b(proj, _rpb_table(b_rpb, j), tail_w, i)
        else:
            lambda_init = 0.8 - 0.6 * math.exp(-0.3 * i)
            proj = _in_proj(h, c_w_in, j, _colscale(C_IN, MIX_WIDTH), lambda t: t)
            og, wo, wg, wp = _attn_c(proj, _alibi_table(), c_lambda, c_subln, j, lambda_init, tail_w, i)
        x2d, h = _tail(og, x2d, p3d, wo, wg, wp, norm_post3, norm_pre3, i)
    return x2d.reshape(BATCH, SEQ, D_MODEL)
```

```python
import functools
import math

import jax
import jax.numpy as jnp
import numpy as np
from jax import lax
from jax.experimental import pallas as pl
from jax.experimental.pallas import tpu as pltpu

F32 = jnp.float32
BF16 = jnp.bfloat16

D_MODEL = 2048
BATCH = 4
SEQ = 2048
DEPTH = 4
TOKENS = BATCH * SEQ
PE_DIM = 256
GRID_W = 64
GRID_ROWS = SEQ // GRID_W
N_MIXERS = 3
HEAD_DIM = 128
MIX_WIDTH = D_MODEL
EPS = 1e-6
NEG_INF = -1e30
LOG2E = math.log2(math.e)
Q_SCALE = HEAD_DIM ** -0.5 * LOG2E

A_HEADS = 16
A_KV_HEADS = 4
A_GROUP = A_HEADS // A_KV_HEADS
A_WINDOW = 128
A_Q = A_HEADS * HEAD_DIM
A_KV = A_KV_HEADS * HEAD_DIM
A_IN = A_Q + 2 * A_KV + MIX_WIDTH

B_HEADS = 16
NB_WIN_H = 8
NB_WIN_W = 16
NB_DY = 2 * NB_WIN_H - 1
NB_DX = 2 * NB_WIN_W - 1
B_IN = 3 * B_HEADS * HEAD_DIM + MIX_WIDTH

C_HEADS = 8
C_V_DIM = 2 * HEAD_DIM
C_IN = 4 * MIX_WIDTH

VMEM_LIMIT_BYTES = 52 * 1024 * 1024


def _alibi_slopes(n_heads):
    return [2.0 ** (-8.0 * (h + 1) / n_heads) for h in range(n_heads)]


def _params(*sem, vmem=VMEM_LIMIT_BYTES):
    return pltpu.CompilerParams(dimension_semantics=sem, vmem_limit_bytes=vmem)


def _rms(xf, w):
    ms = jnp.mean(xf * xf, axis=-1, keepdims=True)
    return xf * lax.rsqrt(ms + EPS) * w


def _silu(g):
    half = 0.5 * g
    return half + half * jnp.tanh(half)


def _dot_nt(a, b):
    return lax.dot_general(a, b, (((1,), (1,)), ((), ())), preferred_element_type=F32)


def _prenorm_kernel(x_ref, w_ref, h_ref):
    h_ref[...] = _rms(x_ref[...], w_ref[...]).astype(BF16)


def _prenorm(x2d, norm_pre3):
    tm = 512
    return pl.pallas_call(
        _prenorm_kernel,
        grid=(TOKENS // tm,),
        in_specs=[pl.BlockSpec((tm, D_MODEL), lambda i: (i, 0)),
                  pl.BlockSpec((None, 1, D_MODEL), lambda i: (0, 0, 0))],
        out_specs=pl.BlockSpec((tm, D_MODEL), lambda i: (i, 0)),
        out_shape=jax.ShapeDtypeStruct((TOKENS, D_MODEL), BF16),
        compiler_params=_params("arbitrary"),
        name="prenorm",
    )(x2d, norm_pre3)


PROJ_TM = 1024
PROJ_TN = 1024


def _in_proj_kernel(h_ref, w_ref, cs_ref, o_ref, wb_ref):
    @pl.when(pl.program_id(1) == 0)
    def _():
        wb_ref[...] = w_ref[...].astype(BF16)

    acc = jnp.dot(h_ref[...], wb_ref[...], preferred_element_type=F32)
    o_ref[...] = (acc * cs_ref[...]).astype(o_ref.dtype)


def _in_proj(h, w_all, layer_j, colscale, out_tile):
    n = w_all.shape[2]
    return pl.pallas_call(
        _in_proj_kernel,
        grid=(n // PROJ_TN, TOKENS // PROJ_TM),
        in_specs=[pl.BlockSpec((PROJ_TM, D_MODEL), lambda j, i: (i, 0)),
                  pl.BlockSpec((None, D_MODEL, PROJ_TN), lambda j, i: (layer_j, 0, j)),
                  pl.BlockSpec((1, PROJ_TN), lambda j, i: (0, j))],
        out_specs=pl.BlockSpec((PROJ_TM, PROJ_TN), lambda j, i: (i, out_tile(j))),
        out_shape=jax.ShapeDtypeStruct((TOKENS, n), BF16),
        scratch_shapes=[pltpu.VMEM((D_MODEL, PROJ_TN), BF16)],
        compiler_params=_params("arbitrary", "arbitrary"),
        name="in_proj",
    )(h, w_all, colscale)


def _colscale(n, n_q):
    cs = np.ones((1, n), np.float32)
    cs[:, :n_q] = Q_SCALE
    return jnp.asarray(cs)


def _rider_specs(layer, nsteps, step_of):
    rows = D_MODEL // nsteps
    slab_in = pl.BlockSpec((None, rows, D_MODEL), lambda *g: (layer, step_of(*g), 0))
    slab_out = pl.BlockSpec((rows, D_MODEL), lambda *g: (step_of(*g), 0))
    in_specs = [slab_in, slab_in, pl.BlockSpec((None, PE_DIM, D_MODEL), lambda *g: (layer, 0, 0))]
    out_specs = [slab_out, slab_out, pl.BlockSpec((PE_DIM, D_MODEL), lambda *g: (0, 0))]
    out_shape = [jax.ShapeDtypeStruct((D_MODEL, D_MODEL), BF16), jax.ShapeDtypeStruct((D_MODEL, D_MODEL), BF16),
                 jax.ShapeDtypeStruct((PE_DIM, D_MODEL), BF16)]
    return in_specs, out_specs, out_shape


def _rider_body(step, wo_f, wg_f, wp_f, wo_b, wg_b, wp_b):
    @pl.when(step == 0)
    def _():
        wp_b[...] = wp_f[...].astype(BF16)

    wo_b[...] = wo_f[...].astype(BF16)
    wg_b[...] = wg_f[...].astype(BF16)


A_TQ = 256
A_KEYS = 3 * A_WINDOW
A_AHEAD = 1


def _attn_a_kernel(sink_ref, q_ref, kp_ref, km_ref, kn_ref, vp_ref, vm_ref, vn_ref, g_ref, wo_f, wg_f, wp_f,
                   o_ref, wo_b, wg_b, wp_b):
    n = pl.program_id(1)
    _rider_body(pl.program_id(0) * (SEQ // A_TQ) + n, wo_f, wg_f, wp_f, wo_b, wg_b, wp_b)
    nsub = A_TQ // A_WINDOW
    qi = lax.broadcasted_iota(jnp.int32, (A_WINDOW, A_WINDOW), 0)
    kr = lax.broadcasted_iota(jnp.int32, (A_WINDOW, A_WINDOW), 1)
    d_prev = (qi + A_WINDOW - kr).astype(F32)
    d_mid = jnp.abs(qi - kr).astype(F32)
    d_next = (kr + A_WINDOW - qi).astype(F32)
    slopes = _alibi_slopes(A_HEADS)
    ones = jnp.ones((A_TQ + 2 * A_WINDOW, HEAD_DIM), BF16)
    kcat, vcat = [], []
    for kv in range(A_KV_HEADS):
        c = slice(kv * HEAD_DIM, (kv + 1) * HEAD_DIM)
        kcat.append(jnp.concatenate([kp_ref[:, c], km_ref[:, c], kn_ref[:, c]], axis=0))
        vrows = jnp.concatenate([vp_ref[:, c], vm_ref[:, c], vn_ref[:, c]], axis=0)
        vcat.append(jnp.concatenate([vrows, ones], axis=1))

    def heads_of(kv):
        return [kv * A_GROUP + g for g in range(A_GROUP)]

    def scores(kv, j):
        rows = slice(j * A_WINDOW, (j + 1) * A_WINDOW)
        qs = jnp.concatenate([q_ref[rows, h * HEAD_DIM:(h + 1) * HEAD_DIM] for h in heads_of(kv)], axis=0)
        return _dot_nt(qs, kcat[kv][j * A_WINDOW:j * A_WINDOW + A_KEYS])

    def finish(kv, j, s):
        rows = slice(j * A_WINDOW, (j + 1) * A_WINDOW)
        off_prev = jnp.where(n > 0, 0, A_WINDOW) if j == 0 else 0
        off_next = jnp.where(n < SEQ // A_TQ - 1, 0, A_WINDOW) if j == nsub - 1 else 0
        ok_prev = kr >= qi + off_prev
        ok_next = kr <= qi - off_next
        probs, sinks = [], []
        for g, h in enumerate(heads_of(kv)):
            sg = s[g * A_WINDOW:(g + 1) * A_WINDOW]
            ch = slopes[h] * LOG2E
            s0 = jnp.where(ok_prev, sg[:, :A_WINDOW] - ch * d_prev, NEG_INF)
            s1 = sg[:, A_WINDOW:2 * A_WINDOW] - ch * d_mid
            s2 = jnp.where(ok_next, sg[:, 2 * A_WINDOW:] - ch * d_next, NEG_INF)
            sink = sink_ref[h] * LOG2E
            m = jnp.max(jnp.maximum(jnp.maximum(s0, s1), s2), axis=-1, keepdims=True)
            m = jnp.maximum(m, sink)
            sinks.append(jnp.exp2(sink - m))
            probs.append(jnp.concatenate([jnp.exp2(s0 - m), jnp.exp2(s1 - m), jnp.exp2(s2 - m)],
                                         axis=-1).astype(BF16))
        ol = jnp.dot(jnp.concatenate(probs, axis=0), vcat[kv][j * A_WINDOW:j * A_WINDOW + A_KEYS],
                     preferred_element_type=F32)
        for g, h in enumerate(heads_of(kv)):
            hc = slice(h * HEAD_DIM, (h + 1) * HEAD_DIM)
            og = ol[g * A_WINDOW:(g + 1) * A_WINDOW]
            o = og[:, :HEAD_DIM] / (og[:, HEAD_DIM:] + sinks[g])
            o_ref[rows, hc] = (o * _silu(g_ref[rows, hc].astype(F32))).astype(BF16)

    units = [(kv, j) for kv in range(A_KV_HEADS) for j in range(nsub)]
    pending = [scores(*unit) for unit in units[:A_AHEAD]]
    for u, unit in enumerate(units):
        if u + A_AHEAD < len(units):
            pending.append(scores(*units[u + A_AHEAD]))
        finish(*unit, pending.pop(0))


def _attn_a(proj, sink, tail_w, layer):
    nq = SEQ // A_TQ
    r_in, r_out, r_shape = _rider_specs(layer, BATCH * nq, lambda b, n: b * nq + n)
    r128 = A_TQ // A_WINDOW
    nb128 = SEQ // A_WINDOW
    kcol = (MIX_WIDTH + A_Q) // A_KV
    vcol = kcol + 1

    def halo(col, which):
        if which == "prev":
            return pl.BlockSpec((A_WINDOW, A_KV),
                                lambda b, n: (b * nb128 + jnp.maximum(n * r128 - 1, 0), col))
        if which == "next":
            return pl.BlockSpec((A_WINDOW, A_KV),
                                lambda b, n: (b * nb128 + jnp.minimum((n + 1) * r128, nb128 - 1), col))
        return pl.BlockSpec((A_TQ, A_KV), lambda b, n: (b * nq + n, col))

    return pl.pallas_call(
        _attn_a_kernel,
        grid=(BATCH, nq),
        in_specs=[pl.BlockSpec(memory_space=pltpu.SMEM),
                  pl.BlockSpec((A_TQ, A_Q), lambda b, n: (b * nq + n, 1)),
                  halo(kcol, "prev"), halo(kcol, "main"), halo(kcol, "next"),
                  halo(vcol, "prev"), halo(vcol, "main"), halo(vcol, "next"),
                  pl.BlockSpec((A_TQ, MIX_WIDTH), lambda b, n: (b * nq + n, 0))] + r_in,
        out_specs=[pl.BlockSpec((A_TQ, MIX_WIDTH), lambda b, n: (b * nq + n, 0))] + r_out,
        out_shape=[jax.ShapeDtypeStruct((TOKENS, MIX_WIDTH), BF16)] + r_shape,
        compiler_params=_params("arbitrary", "arbitrary"),
        name="attn_window",
    )(sink, proj, proj, proj, proj, proj, proj, proj, proj, *tail_w)


NB_KEYS = NB_WIN_H * GRID_W


def _rpb_table_kernel(rpb_ref, out_ref):
    h = pl.program_id(0)
    wq = lax.broadcasted_iota(jnp.int32, (GRID_W, GRID_W), 0)
    wk = lax.broadcasted_iota(jnp.int32, (GRID_W, GRID_W), 1)
    dx = jnp.clip(wk - wq + NB_WIN_W - 1, 0, NB_DX - 1)
    cs = jnp.clip(wq - NB_WIN_W // 2, 0, GRID_W - NB_WIN_W)
    col_valid = (wk >= cs) & (wk < cs + NB_WIN_W)
    base = h * (NB_DY * NB_DX)
    tiles = []
    for dy in range(NB_DY):
        acc = jnp.zeros((GRID_W, GRID_W), F32)
        for d in range(NB_DX):
            acc = jnp.where(dx == d, rpb_ref[base + dy * NB_DX + d], acc)
        tiles.append(jnp.where(col_valid, acc * LOG2E, NEG_INF))
    for v in range(NB_WIN_H):
        out_ref[0, v] = jnp.concatenate([tiles[v + y] for y in range(NB_WIN_H)], axis=-1)


def _rpb_table(rpb_all, layer_j):
    per_layer = B_HEADS * NB_DY * NB_DX
    return pl.pallas_call(
        _rpb_table_kernel,
        grid=(B_HEADS,),
        in_specs=[pl.BlockSpec(memory_space=pltpu.SMEM)],
        out_specs=pl.BlockSpec((1, NB_WIN_H, GRID_W, NB_KEYS), lambda h: (h, 0, 0, 0)),
        out_shape=jax.ShapeDtypeStruct((B_HEADS, NB_WIN_H, GRID_W, NB_KEYS), F32),
        compiler_params=_params("arbitrary"),
        name="rpb_table",
    )(rpb_all.reshape(-1)[layer_j * per_layer:(layer_j + 1) * per_layer])


NB_GROUP = 4
NB_HPS = 2


def _attn_b_kernel(tbl_ref, q_ref, k_ref, v_ref, g_ref, wo_f, wg_f, wp_f, o_ref, wo_b, wg_b, wp_b, kt_scr):
    _rider_body(pl.program_id(0) * BATCH + pl.program_id(1), wo_f, wg_f, wp_f, wo_b, wg_b, wp_b)
    for hh in range(NB_HPS):
        hc = slice(hh * HEAD_DIM, (hh + 1) * HEAD_DIM)
        kt_scr[hh, 0] = k_ref[:, hc].T
        kt_scr[hh, 1, :, :SEQ - 2 * GRID_W] = k_ref[GRID_W:SEQ - GRID_W, hc].T

    def window(r):
        rs = min(max(r - NB_WIN_H // 2, 0), GRID_ROWS - NB_WIN_H)
        return rs, rs - r + NB_WIN_H - 1

    def scores(hh, r):
        rs, dy0 = window(r)
        hc = slice(hh * HEAD_DIM, (hh + 1) * HEAD_DIM)
        q = q_ref[r * GRID_W:(r + 1) * GRID_W, hc]
        odd = rs % 2
        kt = kt_scr[hh, odd, :, (rs - odd) * GRID_W:(rs - odd) * GRID_W + NB_KEYS]
        return jnp.dot(q, kt, preferred_element_type=F32) + tbl_ref[hh, dy0]

    def finish(hh, r, s):
        rs, _ = window(r)
        rows = slice(r * GRID_W, (r + 1) * GRID_W)
        hc = slice(hh * HEAD_DIM, (hh + 1) * HEAD_DIM)
        m = jnp.max(s, axis=-1, keepdims=True)
        p = jnp.exp2(s - m)
        l = jnp.sum(p, axis=-1, keepdims=True)
        v = v_ref[rs * GRID_W:rs * GRID_W + NB_KEYS, hc]
        o = jnp.dot(p.astype(BF16), v, preferred_element_type=F32) / l
        o_ref[rows, hc] = (o * _silu(g_ref[rows, hc].astype(F32))).astype(BF16)

    units = [(hh, r) for hh in range(NB_HPS) for r in range(GRID_ROWS)]
    groups = [units[i:i + NB_GROUP] for i in range(0, len(units), NB_GROUP)]
    pending = [scores(*u) for u in groups[0]]
    for gi, grp in enumerate(groups):
        current = pending
        if gi + 1 < len(groups):
            pending = [scores(*u) for u in groups[gi + 1]]
        for u, s in zip(grp, current):
            finish(*u, s)


def _attn_b(proj, tbl, tail_w, layer):
    steps = B_HEADS // NB_HPS
    r_in, r_out, r_shape = _rider_specs(layer, steps * BATCH, lambda h, b: h * BATCH + b)
    width = NB_HPS * HEAD_DIM

    def heads(off):
        return pl.BlockSpec((SEQ, width), lambda h, b: (b, off + h))

    return pl.pallas_call(
        _attn_b_kernel,
        grid=(steps, BATCH),
        in_specs=[pl.BlockSpec((NB_HPS, NB_WIN_H, GRID_W, NB_KEYS), lambda h, b: (h, 0, 0, 0)),
                  heads(0), heads(steps), heads(2 * steps), heads(3 * steps)] + r_in,
        out_specs=[heads(0)] + r_out,
        out_shape=[jax.ShapeDtypeStruct((TOKENS, MIX_WIDTH), BF16)] + r_shape,
        scratch_shapes=[pltpu.VMEM((NB_HPS, 2, HEAD_DIM, SEQ), BF16)],
        compiler_params=_params("arbitrary", "arbitrary"),
        name="attn_neighbourhood",
    )(tbl, proj, proj, proj, proj, *tail_w)


C_TQ = 256
C_NT = SEQ // C_TQ
C_ND = 2 * C_NT - 1


def _alibi_table_kernel(slope_ref, out_ref):
    h = pl.program_id(0)
    ii = lax.broadcasted_iota(jnp.int32, (C_TQ, C_TQ), 0)
    jj = lax.broadcasted_iota(jnp.int32, (C_TQ, C_TQ), 1)
    for d in range(C_ND):
        dist = jnp.abs((C_NT - 1 - d) * C_TQ + ii - jj).astype(F32)
        out_ref[0, d] = (-slope_ref[h] * LOG2E) * dist


def _alibi_table():
    slopes = jnp.asarray(np.array(_alibi_slopes(C_HEADS), dtype=np.float32))
    return pl.pallas_call(
        _alibi_table_kernel,
        grid=(C_HEADS,),
        in_specs=[pl.BlockSpec(memory_space=pltpu.SMEM)],
        out_specs=pl.BlockSpec((1, C_ND, C_TQ, C_TQ), lambda h: (h, 0, 0, 0)),
        out_shape=jax.ShapeDtypeStruct((C_HEADS, C_ND, C_TQ, C_TQ), F32),
        compiler_params=_params("arbitrary"),
        name="alibi_table",
    )(slopes)


def _attn_c_kernel(lam_ref, subln_ref, tbl_ref, q_ref, k_ref, v_ref, g_ref, wo_f, wg_f, wp_f,
                   o_ref, wo_b, wg_b, wp_b, s_scr, pd_scr, *, lambda_init):
    _rider_body(pl.program_id(0) * C_HEADS + pl.program_id(1), wo_f, wg_f, wp_f, wo_b, wg_b, wp_b)
    lp = lam_ref[...]
    la = jnp.sum(lp[0:1] * lp[1:2], axis=-1, keepdims=True)
    lb = jnp.sum(lp[2:3] * lp[3:4], axis=-1, keepdims=True)
    lam = jnp.exp(la) - jnp.exp(lb) + lambda_init
    maps = [slice(mp * HEAD_DIM, (mp + 1) * HEAD_DIM) for mp in range(2)]
    tiles = [slice(c * C_TQ, (c + 1) * C_TQ) for c in range(C_NT)]

    def scores(n, slot):
        rows = tiles[n]
        s = [_dot_nt(q_ref[rows, mc], k_ref[:, mc]) for mc in maps]
        mx = [None, None]
        for c, tc in enumerate(tiles):
            bias = tbl_ref[0, c - n + C_NT - 1]
            for mp in range(2):
                x = s[mp][:, tc] + bias
                s_scr[slot, mp, :, tc] = x
                mx[mp] = x if mx[mp] is None else jnp.maximum(mx[mp], x)
        return [jnp.max(x, axis=-1, keepdims=True) for x in mx]

    def finish(n, slot, m):
        rows = tiles[n]
        acc = [None, None]
        for tc in tiles:
            for mp in range(2):
                p = jnp.exp2(s_scr[slot, mp, :, tc] - m[mp])
                s_scr[slot, mp, :, tc] = p
                acc[mp] = p if acc[mp] is None else acc[mp] + p
        l = [jnp.sum(x, axis=-1, keepdims=True) for x in acc]
        w0 = 1.0 / l[0]
        w1 = lam / l[1]
        for tc in tiles:
            pd_scr[slot, :, tc] = (s_scr[slot, 0, :, tc] * w0 - s_scr[slot, 1, :, tc] * w1).astype(BF16)
        o = jnp.dot(pd_scr[slot], v_ref[...], preferred_element_type=F32)
        o = _rms(o, subln_ref[...]) * (1.0 - lambda_init)
        o_ref[rows, :] = (o * _silu(g_ref[rows, :].astype(F32))).astype(BF16)

    m_next = scores(0, 0)
    for n in range(C_NT):
        m_cur = m_next
        if n + 1 < C_NT:
            m_next = scores(n + 1, (n + 1) % 2)
        finish(n, n % 2, m_cur)


def _attn_c(proj, tbl, lam_all, subln_all, layer_j, lambda_init, tail_w, layer):
    r_in, r_out, r_shape = _rider_specs(layer, BATCH * C_HEADS, lambda b, h: b * C_HEADS + h)

    def head(off):
        return pl.BlockSpec((SEQ, C_V_DIM), lambda b, h: (b, off + h))

    return pl.pallas_call(
        functools.partial(_attn_c_kernel, lambda_init=lambda_init),
        grid=(BATCH, C_HEADS),
        in_specs=[pl.BlockSpec((None, 4, HEAD_DIM), lambda b, h: (layer_j, 0, 0)),
                  pl.BlockSpec((None, 1, C_V_DIM), lambda b, h: (layer_j, 0, 0)),
                  pl.BlockSpec((1, C_ND, C_TQ, C_TQ), lambda b, h: (h, 0, 0, 0)),
                  head(0), head(C_HEADS), head(2 * C_HEADS), head(3 * C_HEADS)] + r_in,
        out_specs=[head(0)] + r_out,
        out_shape=[jax.ShapeDtypeStruct((TOKENS, MIX_WIDTH), BF16)] + r_shape,
        scratch_shapes=[pltpu.VMEM((2, 2, C_TQ, SEQ), F32), pltpu.VMEM((2, C_TQ, SEQ), BF16)],
        compiler_params=_params("arbitrary", "arbitrary"),
        name="attn_diff",
    )(lam_all, subln_all.reshape(-1, 1, C_V_DIM), tbl, proj, proj, proj, proj, *tail_w)


TAIL_TM = 512
TAIL_SUB = 256


def _tail_kernel(*refs, with_h):
    og_ref, x_ref, p_ref, wo_ref, wg_ref, wp_ref, npost_ref = refs[:7]
    if with_h:
        npre_ref, xo_ref, h_ref = refs[7:]
    else:
        xo_ref, = refs[7:]
    subs = [slice(a, a + TAIL_SUB) for a in range(0, TAIL_TM, TAIL_SUB)]

    def out_proj(r):
        return jnp.dot(og_ref[r, :], wo_ref[...], preferred_element_type=F32)

    def gate_embed(r, y):
        x1 = x_ref[r, :] + _rms(y, npost_ref[...])
        pe = jnp.dot(p_ref[r, :].astype(BF16), wp_ref[...], preferred_element_type=F32)
        gate = jnp.dot(x1.astype(BF16), wg_ref[...], preferred_element_type=F32)
        return x1, gate, pe

    def finish(r, x1, gate, pe):
        x2 = x1 + jax.nn.sigmoid(gate) * pe
        xo_ref[r, :] = x2
        if with_h:
            h_ref[r, :] = _rms(x2, npre_ref[...]).astype(BF16)

    n = len(subs)
    ys, mids = {0: out_proj(subs[0])}, {}
    for t in range(n + 1):
        if t + 1 < n:
            ys[t + 1] = out_proj(subs[t + 1])
        if t < n:
            mids[t] = gate_embed(subs[t], ys.pop(t))
        if t >= 1:
            finish(subs[t - 1], *mids.pop(t - 1))


def _tail(og, x2d, p3d, wo, wg, wp, norm_post3, norm_pre3, layer):
    tm = TAIL_TM
    with_h = layer + 1 < DEPTH
    row = lambda w: pl.BlockSpec((tm, w), lambda i: (i, 0))
    const = lambda a, b, l: pl.BlockSpec((None, a, b), lambda i: (l, 0, 0), pipeline_mode=pl.Buffered(1))
    weight = lambda a, b: pl.BlockSpec((a, b), lambda i: (0, 0), pipeline_mode=pl.Buffered(1))
    in_specs = [row(MIX_WIDTH), row(D_MODEL),
                pl.BlockSpec((None, tm, PE_DIM), lambda i: (layer, i, 0)),
                weight(MIX_WIDTH, D_MODEL), weight(D_MODEL, D_MODEL), weight(PE_DIM, D_MODEL),
                const(1, D_MODEL, layer)]
    args = [og, x2d, p3d, wo, wg, wp, norm_post3]
    out_specs = [row(D_MODEL)]
    out_shape = [jax.ShapeDtypeStruct((TOKENS, D_MODEL), F32)]
    if with_h:
        in_specs.append(const(1, D_MODEL, layer + 1))
        args.append(norm_pre3)
        out_specs.append(row(D_MODEL))
        out_shape.append(jax.ShapeDtypeStruct((TOKENS, D_MODEL), BF16))
    res = pl.pallas_call(
        functools.partial(_tail_kernel, with_h=with_h),
        grid=(TOKENS // tm,),
        in_specs=in_specs,
        out_specs=out_specs,
        out_shape=out_shape,
        compiler_params=_params("arbitrary", vmem=60 * 1024 * 1024),
        name="tail",
    )(*args)
    return (res[0], res[1]) if with_h else (res[0], None)


def kernel(x, p, norm_pre, norm_post, w_out, pe_proj, pe_gate, a_w_in, a_sink, b_w_in, b_rpb,
           c_w_in, c_lambda, c_subln):
    x2d = x.reshape(TOKENS, D_MODEL)
    p3d = p.reshape(DEPTH, TOKENS, PE_DIM)
    norm_pre3 = norm_pre.reshape(DEPTH, 1, D_MODEL)
    norm_post3 = norm_post.reshape(DEPTH, 1, D_MODEL)
    tail_w = (w_out, pe_gate, pe_proj)
    a_tiles = A_IN // PROJ_TN
    g_tiles = MIX_WIDTH // PROJ_TN
    h = _prenorm(x2d, norm_pre3)
    for i in range(DEPTH):
        kind, j = i % N_MIXERS, i // N_MIXERS
        if kind == 0:
            proj = _in_proj(h, a_w_in, j, _colscale(A_IN, A_Q), lambda t: (t + g_tiles) % a_tiles)
            og, wo, wg, wp = _attn_a(proj, a_sink[j], tail_w, i)
        elif kind == 1:
            proj = _in_proj(h, b_w_in, j, _colscale(B_IN, MIX_WIDTH), lambda t: t)
            og, wo, wg, wp = _attn_b(proj, _rpb_table(b_rpb, j), tail_w, i)
        else:
            lambda_init = 0.8 - 0.6 * math.exp(-0.3 * i)
            proj = _in_proj(h, c_w_in, j, _colscale(C_IN, MIX_WIDTH), lambda t: t)
            og, wo, wg, wp = _attn_c(proj, _alibi_table(), c_lambda, c_subln, j, lambda_init, tail_w, i)
        x2d, h = _tail(og, x2d, p3d, wo, wg, wp, norm_post3, norm_pre3, i)
    return x2d.reshape(BATCH, SEQ, D_MODEL)
```

```python
import functools
import math

import jax
import jax.numpy as jnp
import numpy as np
from jax import lax
from jax.experimental import pallas as pl
from jax.experimental.pallas import tpu as pltpu

F32 = jnp.float32
BF16 = jnp.bfloat16

D_MODEL = 2048
BATCH = 4
SEQ = 2048
DEPTH = 4
TOKENS = BATCH * SEQ
PE_DIM = 256
GRID_W = 64
GRID_ROWS = SEQ // GRID_W
N_MIXERS = 3
HEAD_DIM = 128
MIX_WIDTH = D_MODEL
EPS = 1e-6
NEG_INF = -1e30
LOG2E = math.log2(math.e)
Q_SCALE = HEAD_DIM ** -0.5 * LOG2E

A_HEADS = 16
A_KV_HEADS = 4
A_GROUP = A_HEADS // A_KV_HEADS
A_WINDOW = 128
A_Q = A_HEADS * HEAD_DIM
A_KV = A_KV_HEADS * HEAD_DIM
A_IN = A_Q + 2 * A_KV + MIX_WIDTH

B_HEADS = 16
NB_WIN_H = 8
NB_WIN_W = 16
NB_DY = 2 * NB_WIN_H - 1
NB_DX = 2 * NB_WIN_W - 1
B_IN = 3 * B_HEADS * HEAD_DIM + MIX_WIDTH

C_HEADS = 8
C_V_DIM = 2 * HEAD_DIM
C_IN = 4 * MIX_WIDTH

VMEM_LIMIT_BYTES = 52 * 1024 * 1024


def _alibi_slopes(n_heads):
    return [2.0 ** (-8.0 * (h + 1) / n_heads) for h in range(n_heads)]


def _params(*sem, vmem=VMEM_LIMIT_BYTES):
    return pltpu.CompilerParams(dimension_semantics=sem, vmem_limit_bytes=vmem)


def _rms(xf, w):
    ms = jnp.mean(xf * xf, axis=-1, keepdims=True)
    return xf * lax.rsqrt(ms + EPS) * w


def _silu(g):
    half = 0.5 * g
    return half + half * jnp.tanh(half)


def _dot_nt(a, b):
    return lax.dot_general(a, b, (((1,), (1,)), ((), ())), preferred_element_type=F32)


def _prenorm_kernel(x_ref, w_ref, h_ref):
    h_ref[...] = _rms(x_ref[...], w_ref[...]).astype(BF16)


def _prenorm(x2d, norm_pre3):
    tm = 512
    return pl.pallas_call(
        _prenorm_kernel,
        grid=(TOKENS // tm,),
        in_specs=[pl.BlockSpec((tm, D_MODEL), lambda i: (i, 0)),
                  pl.BlockSpec((None, 1, D_MODEL), lambda i: (0, 0, 0))],
        out_specs=pl.BlockSpec((tm, D_MODEL), lambda i: (i, 0)),
        out_shape=jax.ShapeDtypeStruct((TOKENS, D_MODEL), BF16),
        compiler_params=_params("arbitrary"),
        name="prenorm",
    )(x2d, norm_pre3)


PROJ_TM = 1024
PROJ_TN = 1024


def _in_proj_kernel(h_ref, w_ref, cs_ref, o_ref, wb_ref):
    @pl.when(pl.program_id(1) == 0)
    def _():
        wb_ref[...] = w_ref[...].astype(BF16)

    acc = jnp.dot(h_ref[...], wb_ref[...], preferred_element_type=F32)
    o_ref[...] = (acc * cs_ref[...]).astype(o_ref.dtype)


def _in_proj(h, w_all, layer_j, colscale, out_tile):
    n = w_all.shape[2]
    return pl.pallas_call(
        _in_proj_kernel,
        grid=(n // PROJ_TN, TOKENS // PROJ_TM),
        in_specs=[pl.BlockSpec((PROJ_TM, D_MODEL), lambda j, i: (i, 0)),
                  pl.BlockSpec((None, D_MODEL, PROJ_TN), lambda j, i: (layer_j, 0, j)),
                  pl.BlockSpec((1, PROJ_TN), lambda j, i: (0, j))],
        out_specs=pl.BlockSpec((PROJ_TM, PROJ_TN), lambda j, i: (i, out_tile(j))),
        out_shape=jax.ShapeDtypeStruct((TOKENS, n), BF16),
        scratch_shapes=[pltpu.VMEM((D_MODEL, PROJ_TN), BF16)],
        compiler_params=_params("arbitrary", "arbitrary"),
        name="in_proj",
    )(h, w_all, colscale)


def _colscale(n, n_q):
    cs = np.ones((1, n), np.float32)
    cs[:, :n_q] = Q_SCALE
    return jnp.asarray(cs)


def _rider_specs(layer, nsteps, step_of):
    rows = D_MODEL // nsteps
    slab_in = pl.BlockSpec((None, rows, D_MODEL), lambda *g: (layer, step_of(*g), 0))
    slab_out = pl.BlockSpec((rows, D_MODEL), lambda *g: (step_of(*g), 0))
    in_specs = [slab_in, slab_in, pl.BlockSpec((None, PE_DIM, D_MODEL), lambda *g: (layer, 0, 0))]
    out_specs = [slab_out, slab_out, pl.BlockSpec((PE_DIM, D_MODEL), lambda *g: (0, 0))]
    out_shape = [jax.ShapeDtypeStruct((D_MODEL, D_MODEL), BF16), jax.ShapeDtypeStruct((D_MODEL, D_MODEL), BF16),
                 jax.ShapeDtypeStruct((PE_DIM, D_MODEL), BF16)]
    return in_specs, out_specs, out_shape


def _rider_body(step, wo_f, wg_f, wp_f, wo_b, wg_b, wp_b):
    @pl.when(step == 0)
    def _():
        wp_b[...] = wp_f[...].astype(BF16)

    wo_b[...] = wo_f[...].astype(BF16)
    wg_b[...] = wg_f[...].astype(BF16)


A_TQ = 256
A_KEYS = 3 * A_WINDOW
A_AHEAD = 1


def _attn_a_kernel(sink_ref, q_ref, kp_ref, km_ref, kn_ref, vp_ref, vm_ref, vn_ref, g_ref, wo_f, wg_f, wp_f,
                   o_ref, wo_b, wg_b, wp_b):
    n = pl.program_id(1)
    _rider_body(pl.program_id(0) * (SEQ // A_TQ) + n, wo_f, wg_f, wp_f, wo_b, wg_b, wp_b)
    nsub = A_TQ // A_WINDOW
    qi = lax.broadcasted_iota(jnp.int32, (A_WINDOW, A_WINDOW), 0)
    kr = lax.broadcasted_iota(jnp.int32, (A_WINDOW, A_WINDOW), 1)
    d_prev = (qi + A_WINDOW - kr).astype(F32)
    d_mid = jnp.abs(qi - kr).astype(F32)
    d_next = (kr + A_WINDOW - qi).astype(F32)
    slopes = _alibi_slopes(A_HEADS)
    ones = jnp.ones((A_TQ + 2 * A_WINDOW, HEAD_DIM), BF16)
    kcat, vcat = [], []
    for kv in range(A_KV_HEADS):
        c = slice(kv * HEAD_DIM, (kv + 1) * HEAD_DIM)
        kcat.append(jnp.concatenate([kp_ref[:, c], km_ref[:, c], kn_ref[:, c]], axis=0))
        vrows = jnp.concatenate([vp_ref[:, c], vm_ref[:, c], vn_ref[:, c]], axis=0)
        vcat.append(jnp.concatenate([vrows, ones], axis=1))

    def heads_of(kv):
        return [kv * A_GROUP + g for g in range(A_GROUP)]

    def scores(kv, j):
        rows = slice(j * A_WINDOW, (j + 1) * A_WINDOW)
        qs = jnp.concatenate([q_ref[rows, h * HEAD_DIM:(h + 1) * HEAD_DIM] for h in heads_of(kv)], axis=0)
        return _dot_nt(qs, kcat[kv][j * A_WINDOW:j * A_WINDOW + A_KEYS])

    def finish(kv, j, s):
        rows = slice(j * A_WINDOW, (j + 1) * A_WINDOW)
        off_prev = jnp.where(n > 0, 0, A_WINDOW) if j == 0 else 0
        off_next = jnp.where(n < SEQ // A_TQ - 1, 0, A_WINDOW) if j == nsub - 1 else 0
        ok_prev = kr >= qi + off_prev
        ok_next = kr <= qi - off_next
        probs, sinks = [], []
        for g, h in enumerate(heads_of(kv)):
            sg = s[g * A_WINDOW:(g + 1) * A_WINDOW]
            ch = slopes[h] * LOG2E
            s0 = jnp.where(ok_prev, sg[:, :A_WINDOW] - ch * d_prev, NEG_INF)
            s1 = sg[:, A_WINDOW:2 * A_WINDOW] - ch * d_mid
            s2 = jnp.where(ok_next, sg[:, 2 * A_WINDOW:] - ch * d_next, NEG_INF)
            sink = sink_ref[h] * LOG2E
            m = jnp.max(jnp.maximum(jnp.maximum(s0, s1), s2), axis=-1, keepdims=True)
            m = jnp.maximum(m, sink)
            sinks.append(jnp.exp2(sink - m))
            probs.append(jnp.concatenate([jnp.exp2(s0 - m), jnp.exp2(s1 - m), jnp.exp2(s2 - m)],
                                         axis=-1).astype(BF16))
        ol = jnp.dot(jnp.concatenate(probs, axis=0), vcat[kv][j * A_WINDOW:j * A_WINDOW + A_KEYS],
                     preferred_element_type=F32)
        for g, h in enumerate(heads_of(kv)):
            hc = slice(h * HEAD_DIM, (h + 1) * HEAD_DIM)
            og = ol[g * A_WINDOW:(g + 1) * A_WINDOW]
            o = og[:, :HEAD_DIM] / (og[:, HEAD_DIM:] + sinks[g])
            o_ref[rows, hc] = (o * _silu(g_ref[rows, hc].astype(F32))).astype(BF16)

    units = [(kv, j) for kv in range(A_KV_HEADS) for j in range(nsub)]
    pending = [scores(*unit) for unit in units[:A_AHEAD]]
    for u, unit in enumerate(units):
        if u + A_AHEAD < len(units):
            pending.append(scores(*units[u + A_AHEAD]))
        finish(*unit, pending.pop(0))


def _attn_a(proj, sink, tail_w, layer):
    nq = SEQ // A_TQ
    r_in, r_out, r_shape = _rider_specs(layer, BATCH * nq, lambda b, n: b * nq + n)
    r128 = A_TQ // A_WINDOW
    nb128 = SEQ // A_WINDOW
    kcol = (MIX_WIDTH + A_Q) // A_KV
    vcol = kcol + 1

    def halo(col, which):
        if which == "prev":
            return pl.BlockSpec((A_WINDOW, A_KV),
                                lambda b, n: (b * nb128 + jnp.maximum(n * r128 - 1, 0), col))
        if which == "next":
            return pl.BlockSpec((A_WINDOW, A_KV),
                                lambda b, n: (b * nb128 + jnp.minimum((n + 1) * r128, nb128 - 1), col))
        return pl.BlockSpec((A_TQ, A_KV), lambda b, n: (b * nq + n, col))

    return pl.pallas_call(
        _attn_a_kernel,
        grid=(BATCH, nq),
        in_specs=[pl.BlockSpec(memory_space=pltpu.SMEM),
                  pl.BlockSpec((A_TQ, A_Q), lambda b, n: (b * nq + n, 1)),
                  halo(kcol, "prev"), halo(kcol, "main"), halo(kcol, "next"),
                  halo(vcol, "prev"), halo(vcol, "main"), halo(vcol, "next"),
                  pl.BlockSpec((A_TQ, MIX_WIDTH), lambda b, n: (b * nq + n, 0))] + r_in,
        out_specs=[pl.BlockSpec((A_TQ, MIX_WIDTH), lambda b, n: (b * nq + n, 0))] + r_out,
        out_shape=[jax.ShapeDtypeStruct((TOKENS, MIX_WIDTH), BF16)] + r_shape,
        compiler_params=_params("arbitrary", "arbitrary"),
        name="attn_window",
    )(sink, proj, proj, proj, proj, proj, proj, proj, proj, *tail_w)


NB_KEYS = NB_WIN_H * GRID_W


RPB_ROWS, RPB_LANES = 16, 128


def _rpb_table_kernel(rpb_ref, out_ref):
    r = rpb_ref[0]
    u = lax.broadcasted_iota(jnp.int32, (RPB_ROWS, RPB_LANES), 1)
    f = jnp.take_along_axis(r, jnp.clip(u - (GRID_W - NB_WIN_W), 0, NB_DX - 1), axis=1)
    wq = lax.broadcasted_iota(jnp.int32, (GRID_W, RPB_LANES), 0)
    wk = lax.broadcasted_iota(jnp.int32, (GRID_W, RPB_LANES), 1)
    shift = jnp.where(wk < GRID_W, wk - wq + (GRID_W - 1), 0)
    cs = jnp.clip(wq - NB_WIN_W // 2, 0, GRID_W - NB_WIN_W)
    col_valid = (wk >= cs) & (wk < cs + NB_WIN_W)
    tiles = []
    for dy in range(NB_DY):
        row = jnp.broadcast_to(f[dy:dy + 1, :], (GRID_W, RPB_LANES))
        t = jnp.take_along_axis(row, shift, axis=1)
        tiles.append(jnp.where(col_valid, t * LOG2E, NEG_INF)[:, :GRID_W])
    for v in range(NB_WIN_H):
        out_ref[0, v] = jnp.concatenate([tiles[v + y] for y in range(NB_WIN_H)], axis=-1)


def _rpb_table(rpb_all, layer_j):
    padded = jnp.pad(rpb_all[layer_j], ((0, 0), (0, RPB_ROWS - NB_DY), (0, RPB_LANES - NB_DX)))
    return pl.pallas_call(
        _rpb_table_kernel,
        grid=(B_HEADS,),
        in_specs=[pl.BlockSpec((1, RPB_ROWS, RPB_LANES), lambda h: (h, 0, 0))],
        out_specs=pl.BlockSpec((1, NB_WIN_H, GRID_W, NB_KEYS), lambda h: (h, 0, 0, 0)),
        out_shape=jax.ShapeDtypeStruct((B_HEADS, NB_WIN_H, GRID_W, NB_KEYS), F32),
        compiler_params=_params("arbitrary"),
        name="rpb_table",
    )(padded)


NB_GROUP = 4
NB_HPS = 2


def _attn_b_kernel(tbl_ref, q_ref, k_ref, v_ref, g_ref, wo_f, wg_f, wp_f, o_ref, wo_b, wg_b, wp_b, kt_scr):
    _rider_body(pl.program_id(0) * BATCH + pl.program_id(1), wo_f, wg_f, wp_f, wo_b, wg_b, wp_b)
    for hh in range(NB_HPS):
        hc = slice(hh * HEAD_DIM, (hh + 1) * HEAD_DIM)
        kt_scr[hh, 0] = k_ref[:, hc].T
        kt_scr[hh, 1, :, :SEQ - 2 * GRID_W] = k_ref[GRID_W:SEQ - GRID_W, hc].T

    def window(r):
        rs = min(max(r - NB_WIN_H // 2, 0), GRID_ROWS - NB_WIN_H)
        return rs, rs - r + NB_WIN_H - 1

    def scores(hh, r):
        rs, dy0 = window(r)
        hc = slice(hh * HEAD_DIM, (hh + 1) * HEAD_DIM)
        q = q_ref[r * GRID_W:(r + 1) * GRID_W, hc]
        odd = rs % 2
        kt = kt_scr[hh, odd, :, (rs - odd) * GRID_W:(rs - odd) * GRID_W + NB_KEYS]
        return jnp.dot(q, kt, preferred_element_type=F32) + tbl_ref[hh, dy0]

    def finish(hh, r, s):
        rs, _ = window(r)
        rows = slice(r * GRID_W, (r + 1) * GRID_W)
        hc = slice(hh * HEAD_DIM, (hh + 1) * HEAD_DIM)
        m = jnp.max(s, axis=-1, keepdims=True)
        p = jnp.exp2(s - m)
        l = jnp.sum(p, axis=-1, keepdims=True)
        v = v_ref[rs * GRID_W:rs * GRID_W + NB_KEYS, hc]
        o = jnp.dot(p.astype(BF16), v, preferred_element_type=F32) / l
        o_ref[rows, hc] = (o * _silu(g_ref[rows, hc].astype(F32))).astype(BF16)

    units = [(hh, r) for hh in range(NB_HPS) for r in range(GRID_ROWS)]
    groups = [units[i:i + NB_GROUP] for i in range(0, len(units), NB_GROUP)]
    pending = [scores(*u) for u in groups[0]]
    for gi, grp in enumerate(groups):
        current = pending
        if gi + 1 < len(groups):
            pending = [scores(*u) for u in groups[gi + 1]]
        for u, s in zip(grp, current):
            finish(*u, s)


def _attn_b(proj, tbl, tail_w, layer):
    steps = B_HEADS // NB_HPS
    r_in, r_out, r_shape = _rider_specs(layer, steps * BATCH, lambda h, b: h * BATCH + b)
    width = NB_HPS * HEAD_DIM

    def heads(off):
        return pl.BlockSpec((SEQ, width), lambda h, b: (b, off + h))

    return pl.pallas_call(
        _attn_b_kernel,
        grid=(steps, BATCH),
        in_specs=[pl.BlockSpec((NB_HPS, NB_WIN_H, GRID_W, NB_KEYS), lambda h, b: (h, 0, 0, 0)),
                  heads(0), heads(steps), heads(2 * steps), heads(3 * steps)] + r_in,
        out_specs=[heads(0)] + r_out,
        out_shape=[jax.ShapeDtypeStruct((TOKENS, MIX_WIDTH), BF16)] + r_shape,
        scratch_shapes=[pltpu.VMEM((NB_HPS, 2, HEAD_DIM, SEQ), BF16)],
        compiler_params=_params("arbitrary", "arbitrary"),
        name="attn_neighbourhood",
    )(tbl, proj, proj, proj, proj, *tail_w)


C_TQ = 256
C_NT = SEQ // C_TQ
C_ND = 2 * C_NT - 1


def _alibi_table_kernel(slope_ref, out_ref):
    h = pl.program_id(0)
    ii = lax.broadcasted_iota(jnp.int32, (C_TQ, C_TQ), 0)
    jj = lax.broadcasted_iota(jnp.int32, (C_TQ, C_TQ), 1)
    for d in range(C_ND):
        dist = jnp.abs((C_NT - 1 - d) * C_TQ + ii - jj).astype(F32)
        out_ref[0, d] = (-slope_ref[h] * LOG2E) * dist


def _alibi_table():
    slopes = jnp.asarray(np.array(_alibi_slopes(C_HEADS), dtype=np.float32))
    return pl.pallas_call(
        _alibi_table_kernel,
        grid=(C_HEADS,),
        in_specs=[pl.BlockSpec(memory_space=pltpu.SMEM)],
        out_specs=pl.BlockSpec((1, C_ND, C_TQ, C_TQ), lambda h: (h, 0, 0, 0)),
        out_shape=jax.ShapeDtypeStruct((C_HEADS, C_ND, C_TQ, C_TQ), F32),
        compiler_params=_params("arbitrary"),
        name="alibi_table",
    )(slopes)


def _attn_c_kernel(lam_ref, subln_ref, tbl_ref, q_ref, k_ref, v_ref, g_ref, wo_f, wg_f, wp_f,
                   o_ref, wo_b, wg_b, wp_b, s_scr, pd_scr, *, lambda_init):
    _rider_body(pl.program_id(0) * C_HEADS + pl.program_id(1), wo_f, wg_f, wp_f, wo_b, wg_b, wp_b)
    lp = lam_ref[...]
    la = jnp.sum(lp[0:1] * lp[1:2], axis=-1, keepdims=True)
    lb = jnp.sum(lp[2:3] * lp[3:4], axis=-1, keepdims=True)
    lam = jnp.exp(la) - jnp.exp(lb) + lambda_init
    maps = [slice(mp * HEAD_DIM, (mp + 1) * HEAD_DIM) for mp in range(2)]
    tiles = [slice(c * C_TQ, (c + 1) * C_TQ) for c in range(C_NT)]

    def scores(n, slot):
        rows = tiles[n]
        s = [_dot_nt(q_ref[rows, mc], k_ref[:, mc]) for mc in maps]
        mx = [None, None]
        for c, tc in enumerate(tiles):
            bias = tbl_ref[0, c - n + C_NT - 1]
            for mp in range(2):
                x = s[mp][:, tc] + bias
                s_scr[slot, mp, :, tc] = x
                mx[mp] = x if mx[mp] is None else jnp.maximum(mx[mp], x)
        return [jnp.max(x, axis=-1, keepdims=True) for x in mx]

    def finish(n, slot, m):
        rows = tiles[n]
        acc = [None, None]
        for tc in tiles:
            for mp in range(2):
                p = jnp.exp2(s_scr[slot, mp, :, tc] - m[mp])
                s_scr[slot, mp, :, tc] = p
                acc[mp] = p if acc[mp] is None else acc[mp] + p
        l = [jnp.sum(x, axis=-1, keepdims=True) for x in acc]
        w0 = 1.0 / l[0]
        w1 = lam / l[1]
        for tc in tiles:
            pd_scr[slot, :, tc] = (s_scr[slot, 0, :, tc] * w0 - s_scr[slot, 1, :, tc] * w1).astype(BF16)
        o = jnp.dot(pd_scr[slot], v_ref[...], preferred_element_type=F32)
        o = _rms(o, subln_ref[...]) * (1.0 - lambda_init)
        o_ref[rows, :] = (o * _silu(g_ref[rows, :].astype(F32))).astype(BF16)

    m_next = scores(0, 0)
    for n in range(C_NT):
        m_cur = m_next
        if n + 1 < C_NT:
            m_next = scores(n + 1, (n + 1) % 2)
        finish(n, n % 2, m_cur)


def _attn_c(proj, tbl, lam_all, subln_all, layer_j, lambda_init, tail_w, layer):
    r_in, r_out, r_shape = _rider_specs(layer, BATCH * C_HEADS, lambda b, h: b * C_HEADS + h)

    def head(off):
        return pl.BlockSpec((SEQ, C_V_DIM), lambda b, h: (b, off + h))

    return pl.pallas_call(
        functools.partial(_attn_c_kernel, lambda_init=lambda_init),
        grid=(BATCH, C_HEADS),
        in_specs=[pl.BlockSpec((None, 4, HEAD_DIM), lambda b, h: (layer_j, 0, 0)),
                  pl.BlockSpec((None, 1, C_V_DIM), lambda b, h: (layer_j, 0, 0)),
                  pl.BlockSpec((1, C_ND, C_TQ, C_TQ), lambda b, h: (h, 0, 0, 0)),
                  head(0), head(C_HEADS), head(2 * C_HEADS), head(3 * C_HEADS)] + r_in,
        out_specs=[head(0)] + r_out,
        out_shape=[jax.ShapeDtypeStruct((TOKENS, MIX_WIDTH), BF16)] + r_shape,
        scratch_shapes=[pltpu.VMEM((2, 2, C_TQ, SEQ), F32), pltpu.VMEM((2, C_TQ, SEQ), BF16)],
        compiler_params=_params("arbitrary", "arbitrary"),
        name="attn_diff",
    )(lam_all, subln_all.reshape(-1, 1, C_V_DIM), tbl, proj, proj, proj, proj, *tail_w)


TAIL_TM = 512
TAIL_SUB = 256


def _tail_kernel(*refs, with_h):
    og_ref, x_ref, p_ref, wo_ref, wg_ref, wp_ref, npost_ref = refs[:7]
    if with_h:
        npre_ref, xo_ref, h_ref = refs[7:]
    else:
        xo_ref, = refs[7:]
    subs = [slice(a, a + TAIL_SUB) for a in range(0, TAIL_TM, TAIL_SUB)]

    def out_proj(r):
        return jnp.dot(og_ref[r, :], wo_ref[...], preferred_element_type=F32)

    def gate_embed(r, y):
        x1 = x_ref[r, :] + _rms(y, npost_ref[...])
        pe = jnp.dot(p_ref[r, :].astype(BF16), wp_ref[...], preferred_element_type=F32)
        gate = jnp.dot(x1.astype(BF16), wg_ref[...], preferred_element_type=F32)
        return x1, gate, pe

    def finish(r, x1, gate, pe):
        x2 = x1 + jax.nn.sigmoid(gate) * pe
        xo_ref[r, :] = x2
        if with_h:
            h_ref[r, :] = _rms(x2, npre_ref[...]).astype(BF16)

    n = len(subs)
    ys, mids = {0: out_proj(subs[0])}, {}
    for t in range(n + 1):
        if t + 1 < n:
            ys[t + 1] = out_proj(subs[t + 1])
        if t < n:
            mids[t] = gate_embed(subs[t], ys.pop(t))
        if t >= 1:
            finish(subs[t - 1], *mids.pop(t - 1))


def _tail(og, x2d, p3d, wo, wg, wp, norm_post3, norm_pre3, layer):
    tm = TAIL_TM
    with_h = layer + 1 < DEPTH
    row = lambda w: pl.BlockSpec((tm, w), lambda i: (i, 0))
    const = lambda a, b, l: pl.BlockSpec((None, a, b), lambda i: (l, 0, 0), pipeline_mode=pl.Buffered(1))
    weight = lambda a, b: pl.BlockSpec((a, b), lambda i: (0, 0), pipeline_mode=pl.Buffered(1))
    in_specs = [row(MIX_WIDTH), row(D_MODEL),
                pl.BlockSpec((None, tm, PE_DIM), lambda i: (layer, i, 0)),
                weight(MIX_WIDTH, D_MODEL), weight(D_MODEL, D_MODEL), weight(PE_DIM, D_MODEL),
                const(1, D_MODEL, layer)]
    args = [og, x2d, p3d, wo, wg, wp, norm_post3]
    out_specs = [row(D_MODEL)]
    out_shape = [jax.ShapeDtypeStruct((TOKENS, D_MODEL), F32)]
    if with_h:
        in_specs.append(const(1, D_MODEL, layer + 1))
        args.append(norm_pre3)
        out_specs.append(row(D_MODEL))
        out_shape.append(jax.ShapeDtypeStruct((TOKENS, D_MODEL), BF16))
    res = pl.pallas_call(
        functools.partial(_tail_kernel, with_h=with_h),
        grid=(TOKENS // tm,),
        in_specs=in_specs,
        out_specs=out_specs,
        out_shape=out_shape,
        compiler_params=_params("arbitrary", vmem=60 * 1024 * 1024),
        name="tail",
    )(*args)
    return (res[0], res[1]) if with_h else (res[0], None)


def kernel(x, p, norm_pre, norm_post, w_out, pe_proj, pe_gate, a_w_in, a_sink, b_w_in, b_rpb,
           c_w_in, c_lambda, c_subln):
    x2d = x.reshape(TOKENS, D_MODEL)
    p3d = p.reshape(DEPTH, TOKENS, PE_DIM)
    norm_pre3 = norm_pre.reshape(DEPTH, 1, D_MODEL)
    norm_post3 = norm_post.reshape(DEPTH, 1, D_MODEL)
    tail_w = (w_out, pe_gate, pe_proj)
    a_tiles = A_IN // PROJ_TN
    g_tiles = MIX_WIDTH // PROJ_TN
    h = _prenorm(x2d, norm_pre3)
    for i in range(DEPTH):
        kind, j = i % N_MIXERS, i // N_MIXERS
        if kind == 0:
            proj = _in_proj(h, a_w_in, j, _colscale(A_IN, A_Q), lambda t: (t + g_tiles) % a_tiles)
            og, wo, wg, wp = _attn_a(proj, a_sink[j], tail_w, i)
        elif kind == 1:
            proj = _in_proj(h, b_w_in, j, _colscale(B_IN, MIX_WIDTH), lambda t: t)
            og, wo, wg, wp = _attn_b(proj, _rpb_table(b_rpb, j), tail_w, i)
        else:
            lambda_init = 0.8 - 0.6 * math.exp(-0.3 * i)
            proj = _in_proj(h, c_w_in, j, _colscale(C_IN, MIX_WIDTH), lambda t: t)
            og, wo, wg, wp = _attn_c(proj, _alibi_table(), c_lambda, c_subln, j, lambda_init, tail_w, i)
        x2d, h = _tail(og, x2d, p3d, wo, wg, wp, norm_post3, norm_pre3, i)
    return x2d.reshape(BATCH, SEQ, D_MODEL)
```

```python
import functools
import math

import jax
import jax.numpy as jnp
import numpy as np
from jax import lax
from jax.experimental import pallas as pl
from jax.experimental.pallas import tpu as pltpu

F32 = jnp.float32
BF16 = jnp.bfloat16

D_MODEL = 2048
BATCH = 4
SEQ = 2048
DEPTH = 4
TOKENS = BATCH * SEQ
PE_DIM = 256
GRID_W = 64
GRID_ROWS = SEQ // GRID_W
N_MIXERS = 3
HEAD_DIM = 128
MIX_WIDTH = D_MODEL
EPS = 1e-6
NEG_INF = -1e30
LOG2E = math.log2(math.e)
Q_SCALE = HEAD_DIM ** -0.5 * LOG2E

A_HEADS = 16
A_KV_HEADS = 4
A_GROUP = A_HEADS // A_KV_HEADS
A_WINDOW = 128
A_Q = A_HEADS * HEAD_DIM
A_KV = A_KV_HEADS * HEAD_DIM
A_IN = A_Q + 2 * A_KV + MIX_WIDTH

B_HEADS = 16
NB_WIN_H = 8
NB_WIN_W = 16
NB_DY = 2 * NB_WIN_H - 1
NB_DX = 2 * NB_WIN_W - 1
B_IN = 3 * B_HEADS * HEAD_DIM + MIX_WIDTH

C_HEADS = 8
C_V_DIM = 2 * HEAD_DIM
C_IN = 4 * MIX_WIDTH

VMEM_LIMIT_BYTES = 52 * 1024 * 1024


def _alibi_slopes(n_heads):
    return [2.0 ** (-8.0 * (h + 1) / n_heads) for h in range(n_heads)]


def _params(*sem, vmem=VMEM_LIMIT_BYTES):
    return pltpu.CompilerParams(dimension_semantics=sem, vmem_limit_bytes=vmem)


def _rms(xf, w):
    ms = jnp.mean(xf * xf, axis=-1, keepdims=True)
    return xf * lax.rsqrt(ms + EPS) * w


def _silu(g):
    half = 0.5 * g
    return half + half * jnp.tanh(half)


def _dot_nt(a, b):
    return lax.dot_general(a, b, (((1,), (1,)), ((), ())), preferred_element_type=F32)


def _prenorm_kernel(x_ref, w_ref, h_ref):
    h_ref[...] = _rms(x_ref[...], w_ref[...]).astype(BF16)


def _prenorm(x2d, norm_pre3):
    tm = 512
    return pl.pallas_call(
        _prenorm_kernel,
        grid=(TOKENS // tm,),
        in_specs=[pl.BlockSpec((tm, D_MODEL), lambda i: (i, 0)),
                  pl.BlockSpec((None, 1, D_MODEL), lambda i: (0, 0, 0))],
        out_specs=pl.BlockSpec((tm, D_MODEL), lambda i: (i, 0)),
        out_shape=jax.ShapeDtypeStruct((TOKENS, D_MODEL), BF16),
        compiler_params=_params("arbitrary"),
        name="prenorm",
    )(x2d, norm_pre3)


PROJ_TM = 1024
PROJ_TN = 1024


def _in_proj_kernel(h_ref, w_ref, cs_ref, o_ref, wb_ref):
    def project():
        acc = jnp.dot(h_ref[...], wb_ref[...], preferred_element_type=F32)
        o_ref[...] = (acc * cs_ref[...]).astype(o_ref.dtype)

    @pl.when(pl.program_id(1) == 0)
    def _():
        wb_ref[...] = w_ref[...].astype(BF16)
        project()

    @pl.when(pl.program_id(1) != 0)
    def _():
        project()


def _in_proj(h, w_all, layer_j, colscale, out_tile):
    n = w_all.shape[2]
    return pl.pallas_call(
        _in_proj_kernel,
        grid=(n // PROJ_TN, TOKENS // PROJ_TM),
        in_specs=[pl.BlockSpec((PROJ_TM, D_MODEL), lambda j, i: (i, 0)),
                  pl.BlockSpec((None, D_MODEL, PROJ_TN), lambda j, i: (layer_j, 0, j)),
                  pl.BlockSpec((1, PROJ_TN), lambda j, i: (0, j))],
        out_specs=pl.BlockSpec((PROJ_TM, PROJ_TN), lambda j, i: (i, out_tile(j))),
        out_shape=jax.ShapeDtypeStruct((TOKENS, n), BF16),
        scratch_shapes=[pltpu.VMEM((D_MODEL, PROJ_TN), BF16)],
        compiler_params=_params("arbitrary", "arbitrary"),
        name="in_proj",
    )(h, w_all, colscale)


def _colscale(n, n_q):
    cs = np.ones((1, n), np.float32)
    cs[:, :n_q] = Q_SCALE
    return jnp.asarray(cs)


def _rider_specs(layer, nsteps, step_of):
    rows = D_MODEL // nsteps
    slab_in = pl.BlockSpec((None, rows, D_MODEL), lambda *g: (layer, step_of(*g), 0))
    slab_out = pl.BlockSpec((rows, D_MODEL), lambda *g: (step_of(*g), 0))
    in_specs = [slab_in, slab_in, pl.BlockSpec((None, PE_DIM, D_MODEL), lambda *g: (layer, 0, 0))]
    out_specs = [slab_out, slab_out, pl.BlockSpec((PE_DIM, D_MODEL), lambda *g: (0, 0))]
    out_shape = [jax.ShapeDtypeStruct((D_MODEL, D_MODEL), BF16), jax.ShapeDtypeStruct((D_MODEL, D_MODEL), BF16),
                 jax.ShapeDtypeStruct((PE_DIM, D_MODEL), BF16)]
    return in_specs, out_specs, out_shape


def _rider_body(step, wo_f, wg_f, wp_f, wo_b, wg_b, wp_b):
    @pl.when(step == 0)
    def _():
        wp_b[...] = wp_f[...].astype(BF16)

    wo_b[...] = wo_f[...].astype(BF16)
    wg_b[...] = wg_f[...].astype(BF16)


A_TQ = 256
A_KEYS = 3 * A_WINDOW
A_AHEAD = 1


def _attn_a_kernel(sink_ref, q_ref, kp_ref, km_ref, kn_ref, vp_ref, vm_ref, vn_ref, g_ref, wo_f, wg_f, wp_f,
                   o_ref, wo_b, wg_b, wp_b):
    n = pl.program_id(1)
    _rider_body(pl.program_id(0) * (SEQ // A_TQ) + n, wo_f, wg_f, wp_f, wo_b, wg_b, wp_b)
    nsub = A_TQ // A_WINDOW
    qi = lax.broadcasted_iota(jnp.int32, (A_WINDOW, A_WINDOW), 0)
    kr = lax.broadcasted_iota(jnp.int32, (A_WINDOW, A_WINDOW), 1)
    d_prev = (qi + A_WINDOW - kr).astype(F32)
    d_mid = jnp.abs(qi - kr).astype(F32)
    d_next = (kr + A_WINDOW - qi).astype(F32)
    slopes = _alibi_slopes(A_HEADS)
    ones = jnp.ones((A_TQ + 2 * A_WINDOW, HEAD_DIM), BF16)
    kcat, vcat = [], []
    for kv in range(A_KV_HEADS):
        c = slice(kv * HEAD_DIM, (kv + 1) * HEAD_DIM)
        kcat.append(jnp.concatenate([kp_ref[:, c], km_ref[:, c], kn_ref[:, c]], axis=0))
        vrows = jnp.concatenate([vp_ref[:, c], vm_ref[:, c], vn_ref[:, c]], axis=0)
        vcat.append(jnp.concatenate([vrows, ones], axis=1))

    def heads_of(kv):
        return [kv * A_GROUP + g for g in range(A_GROUP)]

    def scores(kv, j):
        rows = slice(j * A_WINDOW, (j + 1) * A_WINDOW)
        qs = jnp.concatenate([q_ref[rows, h * HEAD_DIM:(h + 1) * HEAD_DIM] for h in heads_of(kv)], axis=0)
        return _dot_nt(qs, kcat[kv][j * A_WINDOW:j * A_WINDOW + A_KEYS])

    def finish(kv, j, s):
        rows = slice(j * A_WINDOW, (j + 1) * A_WINDOW)
        off_prev = jnp.where(n > 0, 0, A_WINDOW) if j == 0 else 0
        off_next = jnp.where(n < SEQ // A_TQ - 1, 0, A_WINDOW) if j == nsub - 1 else 0
        ok_prev = kr >= qi + off_prev
        ok_next = kr <= qi - off_next
        probs, sinks = [], []
        for g, h in enumerate(heads_of(kv)):
            sg = s[g * A_WINDOW:(g + 1) * A_WINDOW]
            ch = slopes[h] * LOG2E
            s0 = jnp.where(ok_prev, sg[:, :A_WINDOW] - ch * d_prev, NEG_INF)
            s1 = sg[:, A_WINDOW:2 * A_WINDOW] - ch * d_mid
            s2 = jnp.where(ok_next, sg[:, 2 * A_WINDOW:] - ch * d_next, NEG_INF)
            sink = sink_ref[h] * LOG2E
            m = jnp.max(jnp.maximum(jnp.maximum(s0, s1), s2), axis=-1, keepdims=True)
            m = jnp.maximum(m, sink)
            sinks.append(jnp.exp2(sink - m))
            probs.append(jnp.concatenate([jnp.exp2(s0 - m), jnp.exp2(s1 - m), jnp.exp2(s2 - m)],
                                         axis=-1).astype(BF16))
        ol = jnp.dot(jnp.concatenate(probs, axis=0), vcat[kv][j * A_WINDOW:j * A_WINDOW + A_KEYS],
                     preferred_element_type=F32)
        for g, h in enumerate(heads_of(kv)):
            hc = slice(h * HEAD_DIM, (h + 1) * HEAD_DIM)
            og = ol[g * A_WINDOW:(g + 1) * A_WINDOW]
            o = og[:, :HEAD_DIM] / (og[:, HEAD_DIM:] + sinks[g])
            o_ref[rows, hc] = (o * _silu(g_ref[rows, hc].astype(F32))).astype(BF16)

    units = [(kv, j) for kv in range(A_KV_HEADS) for j in range(nsub)]
    pending = [scores(*unit) for unit in units[:A_AHEAD]]
    for u, unit in enumerate(units):
        if u + A_AHEAD < len(units):
            pending.append(scores(*units[u + A_AHEAD]))
        finish(*unit, pending.pop(0))


def _attn_a(proj, sink, tail_w, layer):
    nq = SEQ // A_TQ
    r_in, r_out, r_shape = _rider_specs(layer, BATCH * nq, lambda b, n: b * nq + n)
    r128 = A_TQ // A_WINDOW
    nb128 = SEQ // A_WINDOW
    kcol = (MIX_WIDTH + A_Q) // A_KV
    vcol = kcol + 1

    def halo(col, which):
        if which == "prev":
            return pl.BlockSpec((A_WINDOW, A_KV),
                                lambda b, n: (b * nb128 + jnp.maximum(n * r128 - 1, 0), col))
        if which == "next":
            return pl.BlockSpec((A_WINDOW, A_KV),
                                lambda b, n: (b * nb128 + jnp.minimum((n + 1) * r128, nb128 - 1), col))
        return pl.BlockSpec((A_TQ, A_KV), lambda b, n: (b * nq + n, col))

    return pl.pallas_call(
        _attn_a_kernel,
        grid=(BATCH, nq),
        in_specs=[pl.BlockSpec(memory_space=pltpu.SMEM),
                  pl.BlockSpec((A_TQ, A_Q), lambda b, n: (b * nq + n, 1)),
                  halo(kcol, "prev"), halo(kcol, "main"), halo(kcol, "next"),
                  halo(vcol, "prev"), halo(vcol, "main"), halo(vcol, "next"),
                  pl.BlockSpec((A_TQ, MIX_WIDTH), lambda b, n: (b * nq + n, 0))] + r_in,
        out_specs=[pl.BlockSpec((A_TQ, MIX_WIDTH), lambda b, n: (b * nq + n, 0))] + r_out,
        out_shape=[jax.ShapeDtypeStruct((TOKENS, MIX_WIDTH), BF16)] + r_shape,
        compiler_params=_params("arbitrary", "arbitrary"),
        name="attn_window",
    )(sink, proj, proj, proj, proj, proj, proj, proj, proj, *tail_w)


NB_KEYS = NB_WIN_H * GRID_W


RPB_ROWS, RPB_LANES = 16, 128


def _rpb_table_kernel(rpb_ref, out_ref):
    r = rpb_ref[0]
    u = lax.broadcasted_iota(jnp.int32, (RPB_ROWS, RPB_LANES), 1)
    f = jnp.take_along_axis(r, jnp.clip(u - (GRID_W - NB_WIN_W), 0, NB_DX - 1), axis=1)
    wq = lax.broadcasted_iota(jnp.int32, (GRID_W, RPB_LANES), 0)
    wk = lax.broadcasted_iota(jnp.int32, (GRID_W, RPB_LANES), 1)
    shift = jnp.where(wk < GRID_W, wk - wq + (GRID_W - 1), 0)
    cs = jnp.clip(wq - NB_WIN_W // 2, 0, GRID_W - NB_WIN_W)
    col_valid = (wk >= cs) & (wk < cs + NB_WIN_W)
    tiles = []
    for dy in range(NB_DY):
        row = jnp.broadcast_to(f[dy:dy + 1, :], (GRID_W, RPB_LANES))
        t = jnp.take_along_axis(row, shift, axis=1)
        tiles.append(jnp.where(col_valid, t * LOG2E, NEG_INF)[:, :GRID_W])
    for v in range(NB_WIN_H):
        out_ref[0, v] = jnp.concatenate([tiles[v + y] for y in range(NB_WIN_H)], axis=-1)


def _rpb_table(rpb_all, layer_j):
    padded = jnp.pad(rpb_all[layer_j], ((0, 0), (0, RPB_ROWS - NB_DY), (0, RPB_LANES - NB_DX)))
    return pl.pallas_call(
        _rpb_table_kernel,
        grid=(B_HEADS,),
        in_specs=[pl.BlockSpec((1, RPB_ROWS, RPB_LANES), lambda h: (h, 0, 0))],
        out_specs=pl.BlockSpec((1, NB_WIN_H, GRID_W, NB_KEYS), lambda h: (h, 0, 0, 0)),
        out_shape=jax.ShapeDtypeStruct((B_HEADS, NB_WIN_H, GRID_W, NB_KEYS), F32),
        compiler_params=_params("arbitrary"),
        name="rpb_table",
    )(padded)


NB_GROUP = 4
NB_HPS = 2


def _attn_b_kernel(tbl_ref, q_ref, k_ref, v_ref, g_ref, wo_f, wg_f, wp_f, o_ref, wo_b, wg_b, wp_b, kt_scr):
    _rider_body(pl.program_id(0) * BATCH + pl.program_id(1), wo_f, wg_f, wp_f, wo_b, wg_b, wp_b)
    for hh in range(NB_HPS):
        hc = slice(hh * HEAD_DIM, (hh + 1) * HEAD_DIM)
        kt_scr[hh, 0] = k_ref[:, hc].T
        kt_scr[hh, 1, :, :SEQ - 2 * GRID_W] = k_ref[GRID_W:SEQ - GRID_W, hc].T

    def window(r):
        rs = min(max(r - NB_WIN_H // 2, 0), GRID_ROWS - NB_WIN_H)
        return rs, rs - r + NB_WIN_H - 1

    def scores(hh, r):
        rs, dy0 = window(r)
        hc = slice(hh * HEAD_DIM, (hh + 1) * HEAD_DIM)
        q = q_ref[r * GRID_W:(r + 1) * GRID_W, hc]
        odd = rs % 2
        kt = kt_scr[hh, odd, :, (rs - odd) * GRID_W:(rs - odd) * GRID_W + NB_KEYS]
        return jnp.dot(q, kt, preferred_element_type=F32) + tbl_ref[hh, dy0]

    def finish(hh, r, s):
        rs, _ = window(r)
        rows = slice(r * GRID_W, (r + 1) * GRID_W)
        hc = slice(hh * HEAD_DIM, (hh + 1) * HEAD_DIM)
        m = jnp.max(s, axis=-1, keepdims=True)
        p = jnp.exp2(s - m)
        l = jnp.sum(p, axis=-1, keepdims=True)
        v = v_ref[rs * GRID_W:rs * GRID_W + NB_KEYS, hc]
        o = jnp.dot(p.astype(BF16), v, preferred_element_type=F32) / l
        o_ref[rows, hc] = (o * _silu(g_ref[rows, hc].astype(F32))).astype(BF16)

    units = [(hh, r) for hh in range(NB_HPS) for r in range(GRID_ROWS)]
    groups = [units[i:i + NB_GROUP] for i in range(0, len(units), NB_GROUP)]
    pending = [scores(*u) for u in groups[0]]
    for gi, grp in enumerate(groups):
        current = pending
        if gi + 1 < len(groups):
            pending = [scores(*u) for u in groups[gi + 1]]
        for u, s in zip(grp, current):
            finish(*u, s)


def _attn_b(proj, tbl, tail_w, layer):
    steps = B_HEADS // NB_HPS
    r_in, r_out, r_shape = _rider_specs(layer, steps * BATCH, lambda h, b: h * BATCH + b)
    width = NB_HPS * HEAD_DIM

    def heads(off):
        return pl.BlockSpec((SEQ, width), lambda h, b: (b, off + h))

    return pl.pallas_call(
        _attn_b_kernel,
        grid=(steps, BATCH),
        in_specs=[pl.BlockSpec((NB_HPS, NB_WIN_H, GRID_W, NB_KEYS), lambda h, b: (h, 0, 0, 0)),
                  heads(0), heads(steps), heads(2 * steps), heads(3 * steps)] + r_in,
        out_specs=[heads(0)] + r_out,
        out_shape=[jax.ShapeDtypeStruct((TOKENS, MIX_WIDTH), BF16)] + r_shape,
        scratch_shapes=[pltpu.VMEM((NB_HPS, 2, HEAD_DIM, SEQ), BF16)],
        compiler_params=_params("arbitrary", "arbitrary"),
        name="attn_neighbourhood",
    )(tbl, proj, proj, proj, proj, *tail_w)


C_TQ = 256
C_NT = SEQ // C_TQ
C_ND = 2 * C_NT - 1


def _alibi_table_kernel(slope_ref, out_ref):
    h = pl.program_id(0)
    ii = lax.broadcasted_iota(jnp.int32, (C_TQ, C_TQ), 0)
    jj = lax.broadcasted_iota(jnp.int32, (C_TQ, C_TQ), 1)
    for d in range(C_ND):
        dist = jnp.abs((C_NT - 1 - d) * C_TQ + ii - jj).astype(F32)
        out_ref[0, d] = (-slope_ref[h] * LOG2E) * dist


def _alibi_table():
    slopes = jnp.asarray(np.array(_alibi_slopes(C_HEADS), dtype=np.float32))
    return pl.pallas_call(
        _alibi_table_kernel,
        grid=(C_HEADS,),
        in_specs=[pl.BlockSpec(memory_space=pltpu.SMEM)],
        out_specs=pl.BlockSpec((1, C_ND, C_TQ, C_TQ), lambda h: (h, 0, 0, 0)),
        out_shape=jax.ShapeDtypeStruct((C_HEADS, C_ND, C_TQ, C_TQ), F32),
        compiler_params=_params("arbitrary"),
        name="alibi_table",
    )(slopes)


def _attn_c_kernel(lam_ref, subln_ref, tbl_ref, q_ref, k_ref, v_ref, g_ref, wo_f, wg_f, wp_f,
                   o_ref, wo_b, wg_b, wp_b, s_scr, pd_scr, *, lambda_init):
    _rider_body(pl.program_id(0) * C_HEADS + pl.program_id(1), wo_f, wg_f, wp_f, wo_b, wg_b, wp_b)
    lp = lam_ref[...]
    la = jnp.sum(lp[0:1] * lp[1:2], axis=-1, keepdims=True)
    lb = jnp.sum(lp[2:3] * lp[3:4], axis=-1, keepdims=True)
    lam = jnp.exp(la) - jnp.exp(lb) + lambda_init
    maps = [slice(mp * HEAD_DIM, (mp + 1) * HEAD_DIM) for mp in range(2)]
    tiles = [slice(c * C_TQ, (c + 1) * C_TQ) for c in range(C_NT)]

    def scores(n, slot):
        rows = tiles[n]
        s = [_dot_nt(q_ref[rows, mc], k_ref[:, mc]) for mc in maps]
        mx = [None, None]
        for c, tc in enumerate(tiles):
            bias = tbl_ref[0, c - n + C_NT - 1]
            for mp in range(2):
                x = s[mp][:, tc] + bias
                s_scr[slot, mp, :, tc] = x
                mx[mp] = x if mx[mp] is None else jnp.maximum(mx[mp], x)
        return [jnp.max(x, axis=-1, keepdims=True) for x in mx]

    def finish(n, slot, m):
        rows = tiles[n]
        acc = [None, None]
        for tc in tiles:
            for mp in range(2):
                p = jnp.exp2(s_scr[slot, mp, :, tc] - m[mp])
                s_scr[slot, mp, :, tc] = p
                acc[mp] = p if acc[mp] is None else acc[mp] + p
        l = [jnp.sum(x, axis=-1, keepdims=True) for x in acc]
        w0 = 1.0 / l[0]
        w1 = lam / l[1]
        for tc in tiles:
            pd_scr[slot, :, tc] = (s_scr[slot, 0, :, tc] * w0 - s_scr[slot, 1, :, tc] * w1).astype(BF16)
        o = jnp.dot(pd_scr[slot], v_ref[...], preferred_element_type=F32)
        o = _rms(o, subln_ref[...]) * (1.0 - lambda_init)
        o_ref[rows, :] = (o * _silu(g_ref[rows, :].astype(F32))).astype(BF16)

    m_next = scores(0, 0)
    for n in range(C_NT):
        m_cur = m_next
        if n + 1 < C_NT:
            m_next = scores(n + 1, (n + 1) % 2)
        finish(n, n % 2, m_cur)


def _attn_c(proj, tbl, lam_all, subln_all, layer_j, lambda_init, tail_w, layer):
    r_in, r_out, r_shape = _rider_specs(layer, BATCH * C_HEADS, lambda b, h: b * C_HEADS + h)

    def head(off):
        return pl.BlockSpec((SEQ, C_V_DIM), lambda b, h: (b, off + h))

    return pl.pallas_call(
        functools.partial(_attn_c_kernel, lambda_init=lambda_init),
        grid=(BATCH, C_HEADS),
        in_specs=[pl.BlockSpec((None, 4, HEAD_DIM), lambda b, h: (layer_j, 0, 0)),
                  pl.BlockSpec((None, 1, C_V_DIM), lambda b, h: (layer_j, 0, 0)),
                  pl.BlockSpec((1, C_ND, C_TQ, C_TQ), lambda b, h: (h, 0, 0, 0)),
                  head(0), head(C_HEADS), head(2 * C_HEADS), head(3 * C_HEADS)] + r_in,
        out_specs=[head(0)] + r_out,
        out_shape=[jax.ShapeDtypeStruct((TOKENS, MIX_WIDTH), BF16)] + r_shape,
        scratch_shapes=[pltpu.VMEM((2, 2, C_TQ, SEQ), F32), pltpu.VMEM((2, C_TQ, SEQ), BF16)],
        compiler_params=_params("arbitrary", "arbitrary"),
        name="attn_diff",
    )(lam_all, subln_all.reshape(-1, 1, C_V_DIM), tbl, proj, proj, proj, proj, *tail_w)


TAIL_TM = 512
TAIL_SUB = 256


def _tail_kernel(*refs, with_h):
    og_ref, x_ref, p_ref, wo_ref, wg_ref, wp_ref, npost_ref = refs[:7]
    if with_h:
        npre_ref, xo_ref, h_ref = refs[7:]
    else:
        xo_ref, = refs[7:]
    subs = [slice(a, a + TAIL_SUB) for a in range(0, TAIL_TM, TAIL_SUB)]

    def out_proj(r):
        return jnp.dot(og_ref[r, :], wo_ref[...], preferred_element_type=F32)

    def gate_embed(r, y):
        x1 = x_ref[r, :] + _rms(y, npost_ref[...])
        pe = jnp.dot(p_ref[r, :].astype(BF16), wp_ref[...], preferred_element_type=F32)
        gate = jnp.dot(x1.astype(BF16), wg_ref[...], preferred_element_type=F32)
        return x1, gate, pe

    def finish(r, x1, gate, pe):
        x2 = x1 + jax.nn.sigmoid(gate) * pe
        xo_ref[r, :] = x2
        if with_h:
            h_ref[r, :] = _rms(x2, npre_ref[...]).astype(BF16)

    n = len(subs)
    ys, mids = {0: out_proj(subs[0])}, {}
    for t in range(n + 1):
        if t + 1 < n:
            ys[t + 1] = out_proj(subs[t + 1])
        if t < n:
            mids[t] = gate_embed(subs[t], ys.pop(t))
        if t >= 1:
            finish(subs[t - 1], *mids.pop(t - 1))


def _tail(og, x2d, p3d, wo, wg, wp, norm_post3, norm_pre3, layer):
    tm = TAIL_TM
    with_h = layer + 1 < DEPTH
    row = lambda w: pl.BlockSpec((tm, w), lambda i: (i, 0))
    const = lambda a, b, l: pl.BlockSpec((None, a, b), lambda i: (l, 0, 0), pipeline_mode=pl.Buffered(1))
    weight = lambda a, b: pl.BlockSpec((a, b), lambda i: (0, 0), pipeline_mode=pl.Buffered(1))
    in_specs = [row(MIX_WIDTH), row(D_MODEL),
                pl.BlockSpec((None, tm, PE_DIM), lambda i: (layer, i, 0)),
                weight(MIX_WIDTH, D_MODEL), weight(D_MODEL, D_MODEL), weight(PE_DIM, D_MODEL),
                const(1, D_MODEL, layer)]
    args = [og, x2d, p3d, wo, wg, wp, norm_post3]
    out_specs = [row(D_MODEL)]
    out_shape = [jax.ShapeDtypeStruct((TOKENS, D_MODEL), F32)]
    if with_h:
        in_specs.append(const(1, D_MODEL, layer + 1))
        args.append(norm_pre3)
        out_specs.append(row(D_MODEL))
        out_shape.append(jax.ShapeDtypeStruct((TOKENS, D_MODEL), BF16))
    res = pl.pallas_call(
        functools.partial(_tail_kernel, with_h=with_h),
        grid=(TOKENS // tm,),
        in_specs=in_specs,
        out_specs=out_specs,
        out_shape=out_shape,
        compiler_params=_params("arbitrary", vmem=60 * 1024 * 1024),
        name="tail",
    )(*args)
    return (res[0], res[1]) if with_h else (res[0], None)


def kernel(x, p, norm_pre, norm_post, w_out, pe_proj, pe_gate, a_w_in, a_sink, b_w_in, b_rpb,
           c_w_in, c_lambda, c_subln):
    x2d = x.reshape(TOKENS, D_MODEL)
    p3d = p.reshape(DEPTH, TOKENS, PE_DIM)
    norm_pre3 = norm_pre.reshape(DEPTH, 1, D_MODEL)
    norm_post3 = norm_post.reshape(DEPTH, 1, D_MODEL)
    tail_w = (w_out, pe_gate, pe_proj)
    a_tiles = A_IN // PROJ_TN
    g_tiles = MIX_WIDTH // PROJ_TN
    h = _prenorm(x2d, norm_pre3)
    for i in range(DEPTH):
        kind, j = i % N_MIXERS, i // N_MIXERS
        if kind == 0:
            proj = _in_proj(h, a_w_in, j, _colscale(A_IN, A_Q), lambda t: (t + g_tiles) % a_tiles)
            og, wo, wg, wp = _attn_a(proj, a_sink[j], tail_w, i)
        elif kind == 1:
            proj = _in_proj(h, b_w_in, j, _colscale(B_IN, MIX_WIDTH), lambda t: t)
            og, wo, wg, wp = _attn_b(proj, _rpb_table(b_rpb, j), tail_w, i)
        else:
            lambda_init = 0.8 - 0.6 * math.exp(-0.3 * i)
            proj = _in_proj(h, c_w_in, j, _colscale(C_IN, MIX_WIDTH), lambda t: t)
            og, wo, wg, wp = _attn_c(proj, _alibi_table(), c_lambda, c_subln, j, lambda_init, tail_w, i)
        x2d, h = _tail(og, x2d, p3d, wo, wg, wp, norm_post3, norm_pre3, i)
    return x2d.reshape(BATCH, SEQ, D_MODEL)
```

```python
import functools
import math

import jax
import jax.numpy as jnp
import numpy as np
from jax import lax
from jax.experimental import pallas as pl
from jax.experimental.pallas import tpu as pltpu

F32 = jnp.float32
BF16 = jnp.bfloat16

D_MODEL = 2048
BATCH = 4
SEQ = 2048
DEPTH = 4
TOKENS = BATCH * SEQ
PE_DIM = 256
GRID_W = 64
GRID_ROWS = SEQ // GRID_W
N_MIXERS = 3
HEAD_DIM = 128
MIX_WIDTH = D_MODEL
EPS = 1e-6
NEG_INF = -1e30
LOG2E = math.log2(math.e)
Q_SCALE = HEAD_DIM ** -0.5 * LOG2E

A_HEADS = 16
A_KV_HEADS = 4
A_GROUP = A_HEADS // A_KV_HEADS
A_WINDOW = 128
A_Q = A_HEADS * HEAD_DIM
A_KV = A_KV_HEADS * HEAD_DIM
A_IN = A_Q + 2 * A_KV + MIX_WIDTH

B_HEADS = 16
NB_WIN_H = 8
NB_WIN_W = 16
NB_DY = 2 * NB_WIN_H - 1
NB_DX = 2 * NB_WIN_W - 1
B_IN = 3 * B_HEADS * HEAD_DIM + MIX_WIDTH

C_HEADS = 8
C_V_DIM = 2 * HEAD_DIM
C_IN = 4 * MIX_WIDTH

VMEM_LIMIT_BYTES = 52 * 1024 * 1024


def _alibi_slopes(n_heads):
    return [2.0 ** (-8.0 * (h + 1) / n_heads) for h in range(n_heads)]


def _params(*sem, vmem=VMEM_LIMIT_BYTES):
    return pltpu.CompilerParams(dimension_semantics=sem, vmem_limit_bytes=vmem)


def _rms(xf, w):
    ms = jnp.mean(xf * xf, axis=-1, keepdims=True)
    return xf * lax.rsqrt(ms + EPS) * w


def _silu(g):
    half = 0.5 * g
    return half + half * jnp.tanh(half)


def _dot_nt(a, b):
    return lax.dot_general(a, b, (((1,), (1,)), ((), ())), preferred_element_type=F32)


def _prenorm_kernel(x_ref, w_ref, h_ref):
    h_ref[...] = _rms(x_ref[...], w_ref[...]).astype(BF16)


def _prenorm(x2d, norm_pre3):
    tm = 512
    return pl.pallas_call(
        _prenorm_kernel,
        grid=(TOKENS // tm,),
        in_specs=[pl.BlockSpec((tm, D_MODEL), lambda i: (i, 0)),
                  pl.BlockSpec((None, 1, D_MODEL), lambda i: (0, 0, 0))],
        out_specs=pl.BlockSpec((tm, D_MODEL), lambda i: (i, 0)),
        out_shape=jax.ShapeDtypeStruct((TOKENS, D_MODEL), BF16),
        compiler_params=_params("arbitrary"),
        name="prenorm",
    )(x2d, norm_pre3)


PROJ_TM = 1024
PROJ_TN = 1024


def _in_proj_kernel(h_ref, w_ref, cs_ref, o_ref, wb_ref):
    def project():
        halves = [slice(a, a + PROJ_TM // 2) for a in (0, PROJ_TM // 2)]
        accs = [jnp.dot(h_ref[r, :], wb_ref[...], preferred_element_type=F32) for r in halves]
        for r, acc in zip(halves, accs):
            o_ref[r, :] = (acc * cs_ref[...]).astype(o_ref.dtype)

    @pl.when(pl.program_id(1) == 0)
    def _():
        wb_ref[...] = w_ref[...].astype(BF16)
        project()

    @pl.when(pl.program_id(1) != 0)
    def _():
        project()


def _in_proj(h, w_all, layer_j, colscale, out_tile):
    n = w_all.shape[2]
    return pl.pallas_call(
        _in_proj_kernel,
        grid=(n // PROJ_TN, TOKENS // PROJ_TM),
        in_specs=[pl.BlockSpec((PROJ_TM, D_MODEL), lambda j, i: (i, 0)),
                  pl.BlockSpec((None, D_MODEL, PROJ_TN), lambda j, i: (layer_j, 0, j)),
                  pl.BlockSpec((1, PROJ_TN), lambda j, i: (0, j))],
        out_specs=pl.BlockSpec((PROJ_TM, PROJ_TN), lambda j, i: (i, out_tile(j))),
        out_shape=jax.ShapeDtypeStruct((TOKENS, n), BF16),
        scratch_shapes=[pltpu.VMEM((D_MODEL, PROJ_TN), BF16)],
        compiler_params=_params("arbitrary", "arbitrary"),
        name="in_proj",
    )(h, w_all, colscale)


def _colscale(n, n_q):
    cs = np.ones((1, n), np.float32)
    cs[:, :n_q] = Q_SCALE
    return jnp.asarray(cs)


def _rider_specs(layer, nsteps, step_of):
    rows = D_MODEL // nsteps
    slab_in = pl.BlockSpec((None, rows, D_MODEL), lambda *g: (layer, step_of(*g), 0))
    slab_out = pl.BlockSpec((rows, D_MODEL), lambda *g: (step_of(*g), 0))
    in_specs = [slab_in, slab_in, pl.BlockSpec((None, PE_DIM, D_MODEL), lambda *g: (layer, 0, 0))]
    out_specs = [slab_out, slab_out, pl.BlockSpec((PE_DIM, D_MODEL), lambda *g: (0, 0))]
    out_shape = [jax.ShapeDtypeStruct((D_MODEL, D_MODEL), BF16), jax.ShapeDtypeStruct((D_MODEL, D_MODEL), BF16),
                 jax.ShapeDtypeStruct((PE_DIM, D_MODEL), BF16)]
    return in_specs, out_specs, out_shape


def _rider_body(step, wo_f, wg_f, wp_f, wo_b, wg_b, wp_b):
    @pl.when(step == 0)
    def _():
        wp_b[...] = wp_f[...].astype(BF16)

    wo_b[...] = wo_f[...].astype(BF16)
    wg_b[...] = wg_f[...].astype(BF16)


A_TQ = 256
A_KEYS = 3 * A_WINDOW
A_AHEAD = 1


def _attn_a_kernel(sink_ref, q_ref, kp_ref, km_ref, kn_ref, vp_ref, vm_ref, vn_ref, g_ref, wo_f, wg_f, wp_f,
                   o_ref, wo_b, wg_b, wp_b):
    n = pl.program_id(1)
    _rider_body(pl.program_id(0) * (SEQ // A_TQ) + n, wo_f, wg_f, wp_f, wo_b, wg_b, wp_b)
    nsub = A_TQ // A_WINDOW
    qi = lax.broadcasted_iota(jnp.int32, (A_WINDOW, A_WINDOW), 0)
    kr = lax.broadcasted_iota(jnp.int32, (A_WINDOW, A_WINDOW), 1)
    d_prev = (qi + A_WINDOW - kr).astype(F32)
    d_mid = jnp.abs(qi - kr).astype(F32)
    d_next = (kr + A_WINDOW - qi).astype(F32)
    slopes = _alibi_slopes(A_HEADS)
    ones = jnp.ones((A_TQ + 2 * A_WINDOW, HEAD_DIM), BF16)
    kcat, vcat = [], []
    for kv in range(A_KV_HEADS):
        c = slice(kv * HEAD_DIM, (kv + 1) * HEAD_DIM)
        kcat.append(jnp.concatenate([kp_ref[:, c], km_ref[:, c], kn_ref[:, c]], axis=0))
        vrows = jnp.concatenate([vp_ref[:, c], vm_ref[:, c], vn_ref[:, c]], axis=0)
        vcat.append(jnp.concatenate([vrows, ones], axis=1))

    def heads_of(kv):
        return [kv * A_GROUP + g for g in range(A_GROUP)]

    def scores(kv, j):
        rows = slice(j * A_WINDOW, (j + 1) * A_WINDOW)
        qs = jnp.concatenate([q_ref[rows, h * HEAD_DIM:(h + 1) * HEAD_DIM] for h in heads_of(kv)], axis=0)
        return _dot_nt(qs, kcat[kv][j * A_WINDOW:j * A_WINDOW + A_KEYS])

    def finish(kv, j, s):
        rows = slice(j * A_WINDOW, (j + 1) * A_WINDOW)
        off_prev = jnp.where(n > 0, 0, A_WINDOW) if j == 0 else 0
        off_next = jnp.where(n < SEQ // A_TQ - 1, 0, A_WINDOW) if j == nsub - 1 else 0
        ok_prev = kr >= qi + off_prev
        ok_next = kr <= qi - off_next
        probs, sinks = [], []
        for g, h in enumerate(heads_of(kv)):
            sg = s[g * A_WINDOW:(g + 1) * A_WINDOW]
            ch = slopes[h] * LOG2E
            s0 = jnp.where(ok_prev, sg[:, :A_WINDOW] - ch * d_prev, NEG_INF)
            s1 = sg[:, A_WINDOW:2 * A_WINDOW] - ch * d_mid
            s2 = jnp.where(ok_next, sg[:, 2 * A_WINDOW:] - ch * d_next, NEG_INF)
            sink = sink_ref[h] * LOG2E
            m = jnp.max(jnp.maximum(jnp.maximum(s0, s1), s2), axis=-1, keepdims=True)
            m = jnp.maximum(m, sink)
            sinks.append(jnp.exp2(sink - m))
            probs.append(jnp.concatenate([jnp.exp2(s0 - m), jnp.exp2(s1 - m), jnp.exp2(s2 - m)],
                                         axis=-1).astype(BF16))
        ol = jnp.dot(jnp.concatenate(probs, axis=0), vcat[kv][j * A_WINDOW:j * A_WINDOW + A_KEYS],
                     preferred_element_type=F32)
        for g, h in enumerate(heads_of(kv)):
            hc = slice(h * HEAD_DIM, (h + 1) * HEAD_DIM)
            og = ol[g * A_WINDOW:(g + 1) * A_WINDOW]
            o = og[:, :HEAD_DIM] / (og[:, HEAD_DIM:] + sinks[g])
            o_ref[rows, hc] = (o * _silu(g_ref[rows, hc].astype(F32))).astype(BF16)

    units = [(kv, j) for kv in range(A_KV_HEADS) for j in range(nsub)]
    pending = [scores(*unit) for unit in units[:A_AHEAD]]
    for u, unit in enumerate(units):
        if u + A_AHEAD < len(units):
            pending.append(scores(*units[u + A_AHEAD]))
        finish(*unit, pending.pop(0))


def _attn_a(proj, sink, tail_w, layer):
    nq = SEQ // A_TQ
    r_in, r_out, r_shape = _rider_specs(layer, BATCH * nq, lambda b, n: b * nq + n)
    r128 = A_TQ // A_WINDOW
    nb128 = SEQ // A_WINDOW
    kcol = (MIX_WIDTH + A_Q) // A_KV
    vcol = kcol + 1

    def halo(col, which):
        if which == "prev":
            return pl.BlockSpec((A_WINDOW, A_KV),
                                lambda b, n: (b * nb128 + jnp.maximum(n * r128 - 1, 0), col))
        if which == "next":
            return pl.BlockSpec((A_WINDOW, A_KV),
                                lambda b, n: (b * nb128 + jnp.minimum((n + 1) * r128, nb128 - 1), col))
        return pl.BlockSpec((A_TQ, A_KV), lambda b, n: (b * nq + n, col))

    return pl.pallas_call(
        _attn_a_kernel,
        grid=(BATCH, nq),
        in_specs=[pl.BlockSpec(memory_space=pltpu.SMEM),
                  pl.BlockSpec((A_TQ, A_Q), lambda b, n: (b * nq + n, 1)),
                  halo(kcol, "prev"), halo(kcol, "main"), halo(kcol, "next"),
                  halo(vcol, "prev"), halo(vcol, "main"), halo(vcol, "next"),
                  pl.BlockSpec((A_TQ, MIX_WIDTH), lambda b, n: (b * nq + n, 0))] + r_in,
        out_specs=[pl.BlockSpec((A_TQ, MIX_WIDTH), lambda b, n: (b * nq + n, 0))] + r_out,
        out_shape=[jax.ShapeDtypeStruct((TOKENS, MIX_WIDTH), BF16)] + r_shape,
        compiler_params=_params("arbitrary", "arbitrary"),
        name="attn_window",
    )(sink, proj, proj, proj, proj, proj, proj, proj, proj, *tail_w)


NB_KEYS = NB_WIN_H * GRID_W


RPB_ROWS, RPB_LANES = 16, 128


def _rpb_table_kernel(rpb_ref, out_ref):
    r = rpb_ref[0]
    u = lax.broadcasted_iota(jnp.int32, (RPB_ROWS, RPB_LANES), 1)
    f = jnp.take_along_axis(r, jnp.clip(u - (GRID_W - NB_WIN_W), 0, NB_DX - 1), axis=1)
    wq = lax.broadcasted_iota(jnp.int32, (GRID_W, RPB_LANES), 0)
    wk = lax.broadcasted_iota(jnp.int32, (GRID_W, RPB_LANES), 1)
    shift = jnp.where(wk < GRID_W, wk - wq + (GRID_W - 1), 0)
    cs = jnp.clip(wq - NB_WIN_W // 2, 0, GRID_W - NB_WIN_W)
    col_valid = (wk >= cs) & (wk < cs + NB_WIN_W)
    tiles = []
    for dy in range(NB_DY):
        row = jnp.broadcast_to(f[dy:dy + 1, :], (GRID_W, RPB_LANES))
        t = jnp.take_along_axis(row, shift, axis=1)
        tiles.append(jnp.where(col_valid, t * LOG2E, NEG_INF)[:, :GRID_W])
    for v in range(NB_WIN_H):
        out_ref[0, v] = jnp.concatenate([tiles[v + y] for y in range(NB_WIN_H)], axis=-1)


def _rpb_table(rpb_all, layer_j):
    padded = jnp.pad(rpb_all[layer_j], ((0, 0), (0, RPB_ROWS - NB_DY), (0, RPB_LANES - NB_DX)))
    return pl.pallas_call(
        _rpb_table_kernel,
        grid=(B_HEADS,),
        in_specs=[pl.BlockSpec((1, RPB_ROWS, RPB_LANES), lambda h: (h, 0, 0))],
        out_specs=pl.BlockSpec((1, NB_WIN_H, GRID_W, NB_KEYS), lambda h: (h, 0, 0, 0)),
        out_shape=jax.ShapeDtypeStruct((B_HEADS, NB_WIN_H, GRID_W, NB_KEYS), F32),
        compiler_params=_params("arbitrary"),
        name="rpb_table",
    )(padded)


NB_GROUP = 4
NB_HPS = 2


def _attn_b_kernel(tbl_ref, q_ref, k_ref, v_ref, g_ref, wo_f, wg_f, wp_f, o_ref, wo_b, wg_b, wp_b, kt_scr):
    _rider_body(pl.program_id(0) * BATCH + pl.program_id(1), wo_f, wg_f, wp_f, wo_b, wg_b, wp_b)
    for hh in range(NB_HPS):
        hc = slice(hh * HEAD_DIM, (hh + 1) * HEAD_DIM)
        kt_scr[hh, 0] = k_ref[:, hc].T
        kt_scr[hh, 1, :, :SEQ - 2 * GRID_W] = k_ref[GRID_W:SEQ - GRID_W, hc].T

    def window(r):
        rs = min(max(r - NB_WIN_H // 2, 0), GRID_ROWS - NB_WIN_H)
        return rs, rs - r + NB_WIN_H - 1

    def scores(hh, r):
        rs, dy0 = window(r)
        hc = slice(hh * HEAD_DIM, (hh + 1) * HEAD_DIM)
        q = q_ref[r * GRID_W:(r + 1) * GRID_W, hc]
        odd = rs % 2
        kt = kt_scr[hh, odd, :, (rs - odd) * GRID_W:(rs - odd) * GRID_W + NB_KEYS]
        return jnp.dot(q, kt, preferred_element_type=F32) + tbl_ref[hh, dy0]

    def finish(hh, r, s):
        rs, _ = window(r)
        rows = slice(r * GRID_W, (r + 1) * GRID_W)
        hc = slice(hh * HEAD_DIM, (hh + 1) * HEAD_DIM)
        m = jnp.max(s, axis=-1, keepdims=True)
        p = jnp.exp2(s - m)
        l = jnp.sum(p, axis=-1, keepdims=True)
        v = v_ref[rs * GRID_W:rs * GRID_W + NB_KEYS, hc]
        o = jnp.dot(p.astype(BF16), v, preferred_element_type=F32) / l
        o_ref[rows, hc] = (o * _silu(g_ref[rows, hc].astype(F32))).astype(BF16)

    units = [(hh, r) for hh in range(NB_HPS) for r in range(GRID_ROWS)]
    groups = [units[i:i + NB_GROUP] for i in range(0, len(units), NB_GROUP)]
    pending = [scores(*u) for u in groups[0]]
    for gi, grp in enumerate(groups):
        current = pending
        if gi + 1 < len(groups):
            pending = [scores(*u) for u in groups[gi + 1]]
        for u, s in zip(grp, current):
            finish(*u, s)


def _attn_b(proj, tbl, tail_w, layer):
    steps = B_HEADS // NB_HPS
    r_in, r_out, r_shape = _rider_specs(layer, steps * BATCH, lambda h, b: h * BATCH + b)
    width = NB_HPS * HEAD_DIM

    def heads(off):
        return pl.BlockSpec((SEQ, width), lambda h, b: (b, off + h))

    return pl.pallas_call(
        _attn_b_kernel,
        grid=(steps, BATCH),
        in_specs=[pl.BlockSpec((NB_HPS, NB_WIN_H, GRID_W, NB_KEYS), lambda h, b: (h, 0, 0, 0)),
                  heads(0), heads(steps), heads(2 * steps), heads(3 * steps)] + r_in,
        out_specs=[heads(0)] + r_out,
        out_shape=[jax.ShapeDtypeStruct((TOKENS, MIX_WIDTH), BF16)] + r_shape,
        scratch_shapes=[pltpu.VMEM((NB_HPS, 2, HEAD_DIM, SEQ), BF16)],
        compiler_params=_params("arbitrary", "arbitrary"),
        name="attn_neighbourhood",
    )(tbl, proj, proj, proj, proj, *tail_w)


C_TQ = 256
C_NT = SEQ // C_TQ
C_ND = 2 * C_NT - 1


def _alibi_table_kernel(slope_ref, out_ref):
    h = pl.program_id(0)
    ii = lax.broadcasted_iota(jnp.int32, (C_TQ, C_TQ), 0)
    jj = lax.broadcasted_iota(jnp.int32, (C_TQ, C_TQ), 1)
    for d in range(C_ND):
        dist = jnp.abs((C_NT - 1 - d) * C_TQ + ii - jj).astype(F32)
        out_ref[0, d] = (-slope_ref[h] * LOG2E) * dist


def _alibi_table():
    slopes = jnp.asarray(np.array(_alibi_slopes(C_HEADS), dtype=np.float32))
    return pl.pallas_call(
        _alibi_table_kernel,
        grid=(C_HEADS,),
        in_specs=[pl.BlockSpec(memory_space=pltpu.SMEM)],
        out_specs=pl.BlockSpec((1, C_ND, C_TQ, C_TQ), lambda h: (h, 0, 0, 0)),
        out_shape=jax.ShapeDtypeStruct((C_HEADS, C_ND, C_TQ, C_TQ), F32),
        compiler_params=_params("arbitrary"),
        name="alibi_table",
    )(slopes)


def _attn_c_kernel(lam_ref, subln_ref, tbl_ref, q_ref, k_ref, v_ref, g_ref, wo_f, wg_f, wp_f,
                   o_ref, wo_b, wg_b, wp_b, s_scr, pd_scr, *, lambda_init):
    _rider_body(pl.program_id(0) * C_HEADS + pl.program_id(1), wo_f, wg_f, wp_f, wo_b, wg_b, wp_b)
    lp = lam_ref[...]
    la = jnp.sum(lp[0:1] * lp[1:2], axis=-1, keepdims=True)
    lb = jnp.sum(lp[2:3] * lp[3:4], axis=-1, keepdims=True)
    lam = jnp.exp(la) - jnp.exp(lb) + lambda_init
    maps = [slice(mp * HEAD_DIM, (mp + 1) * HEAD_DIM) for mp in range(2)]
    tiles = [slice(c * C_TQ, (c + 1) * C_TQ) for c in range(C_NT)]

    def scores(n, slot):
        rows = tiles[n]
        s = [_dot_nt(q_ref[rows, mc], k_ref[:, mc]) for mc in maps]
        mx = [None, None]
        for c, tc in enumerate(tiles):
            bias = tbl_ref[0, c - n + C_NT - 1]
            for mp in range(2):
                x = s[mp][:, tc] + bias
                s_scr[slot, mp, :, tc] = x
                mx[mp] = x if mx[mp] is None else jnp.maximum(mx[mp], x)
        return [jnp.max(x, axis=-1, keepdims=True) for x in mx]

    def finish(n, slot, m):
        rows = tiles[n]
        acc = [None, None]
        for tc in tiles:
            for mp in range(2):
                p = jnp.exp2(s_scr[slot, mp, :, tc] - m[mp])
                s_scr[slot, mp, :, tc] = p
                acc[mp] = p if acc[mp] is None else acc[mp] + p
        l = [jnp.sum(x, axis=-1, keepdims=True) for x in acc]
        w0 = 1.0 / l[0]
        w1 = lam / l[1]
        for tc in tiles:
            pd_scr[slot, :, tc] = (s_scr[slot, 0, :, tc] * w0 - s_scr[slot, 1, :, tc] * w1).astype(BF16)
        o = jnp.dot(pd_scr[slot], v_ref[...], preferred_element_type=F32)
        o = _rms(o, subln_ref[...]) * (1.0 - lambda_init)
        o_ref[rows, :] = (o * _silu(g_ref[rows, :].astype(F32))).astype(BF16)

    m_next = scores(0, 0)
    for n in range(C_NT):
        m_cur = m_next
        if n + 1 < C_NT:
            m_next = scores(n + 1, (n + 1) % 2)
        finish(n, n % 2, m_cur)


def _attn_c(proj, tbl, lam_all, subln_all, layer_j, lambda_init, tail_w, layer):
    r_in, r_out, r_shape = _rider_specs(layer, BATCH * C_HEADS, lambda b, h: b * C_HEADS + h)

    def head(off):
        return pl.BlockSpec((SEQ, C_V_DIM), lambda b, h: (b, off + h))

    return pl.pallas_call(
        functools.partial(_attn_c_kernel, lambda_init=lambda_init),
        grid=(BATCH, C_HEADS),
        in_specs=[pl.BlockSpec((None, 4, HEAD_DIM), lambda b, h: (layer_j, 0, 0)),
                  pl.BlockSpec((None, 1, C_V_DIM), lambda b, h: (layer_j, 0, 0)),
                  pl.BlockSpec((1, C_ND, C_TQ, C_TQ), lambda b, h: (h, 0, 0, 0)),
                  head(0), head(C_HEADS), head(2 * C_HEADS), head(3 * C_HEADS)] + r_in,
        out_specs=[head(0)] + r_out,
        out_shape=[jax.ShapeDtypeStruct((TOKENS, MIX_WIDTH), BF16)] + r_shape,
        scratch_shapes=[pltpu.VMEM((2, 2, C_TQ, SEQ), F32), pltpu.VMEM((2, C_TQ, SEQ), BF16)],
        compiler_params=_params("arbitrary", "arbitrary"),
        name="attn_diff",
    )(lam_all, subln_all.reshape(-1, 1, C_V_DIM), tbl, proj, proj, proj, proj, *tail_w)


TAIL_TM = 512
TAIL_SUB = 256


def _tail_kernel(*refs, with_h):
    og_ref, x_ref, p_ref, wo_ref, wg_ref, wp_ref, npost_ref = refs[:7]
    if with_h:
        npre_ref, xo_ref, h_ref = refs[7:]
    else:
        xo_ref, = refs[7:]
    subs = [slice(a, a + TAIL_SUB) for a in range(0, TAIL_TM, TAIL_SUB)]

    def out_proj(r):
        return jnp.dot(og_ref[r, :], wo_ref[...], preferred_element_type=F32)

    def gate_embed(r, y):
        x1 = x_ref[r, :] + _rms(y, npost_ref[...])
        pe = jnp.dot(p_ref[r, :].astype(BF16), wp_ref[...], preferred_element_type=F32)
        gate = jnp.dot(x1.astype(BF16), wg_ref[...], preferred_element_type=F32)
        return x1, gate, pe

    def finish(r, x1, gate, pe):
        x2 = x1 + jax.nn.sigmoid(gate) * pe
        xo_ref[r, :] = x2
        if with_h:
            h_ref[r, :] = _rms(x2, npre_ref[...]).astype(BF16)

    n = len(subs)
    ys, mids = {0: out_proj(subs[0])}, {}
    for t in range(n + 1):
        if t + 1 < n:
            ys[t + 1] = out_proj(subs[t + 1])
        if t < n:
            mids[t] = gate_embed(subs[t], ys.pop(t))
        if t >= 1:
            finish(subs[t - 1], *mids.pop(t - 1))


def _tail(og, x2d, p3d, wo, wg, wp, norm_post3, norm_pre3, layer):
    tm = TAIL_TM
    with_h = layer + 1 < DEPTH
    row = lambda w: pl.BlockSpec((tm, w), lambda i: (i, 0))
    const = lambda a, b, l: pl.BlockSpec((None, a, b), lambda i: (l, 0, 0), pipeline_mode=pl.Buffered(1))
    weight = lambda a, b: pl.BlockSpec((a, b), lambda i: (0, 0), pipeline_mode=pl.Buffered(1))
    in_specs = [row(MIX_WIDTH), row(D_MODEL),
                pl.BlockSpec((None, tm, PE_DIM), lambda i: (layer, i, 0)),
                weight(MIX_WIDTH, D_MODEL), weight(D_MODEL, D_MODEL), weight(PE_DIM, D_MODEL),
                const(1, D_MODEL, layer)]
    args = [og, x2d, p3d, wo, wg, wp, norm_post3]
    out_specs = [row(D_MODEL)]
    out_shape = [jax.ShapeDtypeStruct((TOKENS, D_MODEL), F32)]
    if with_h:
        in_specs.append(const(1, D_MODEL, layer + 1))
        args.append(norm_pre3)
        out_specs.append(row(D_MODEL))
        out_shape.append(jax.ShapeDtypeStruct((TOKENS, D_MODEL), BF16))
    res = pl.pallas_call(
        functools.partial(_tail_kernel, with_h=with_h),
        grid=(TOKENS // tm,),
        in_specs=in_specs,
        out_specs=out_specs,
        out_shape=out_shape,
        compiler_params=_params("arbitrary", vmem=60 * 1024 * 1024),
        name="tail",
    )(*args)
    return (res[0], res[1]) if with_h else (res[0], None)


def kernel(x, p, norm_pre, norm_post, w_out, pe_proj, pe_gate, a_w_in, a_sink, b_w_in, b_rpb,
           c_w_in, c_lambda, c_subln):
    x2d = x.reshape(TOKENS, D_MODEL)
    p3d = p.reshape(DEPTH, TOKENS, PE_DIM)
    norm_pre3 = norm_pre.reshape(DEPTH, 1, D_MODEL)
    norm_post3 = norm_post.reshape(DEPTH, 1, D_MODEL)
    tail_w = (w_out, pe_gate, pe_proj)
    a_tiles = A_IN // PROJ_TN
    g_tiles = MIX_WIDTH // PROJ_TN
    h = _prenorm(x2d, norm_pre3)
    for i in range(DEPTH):
        kind, j = i % N_MIXERS, i // N_MIXERS
        if kind == 0:
            proj = _in_proj(h, a_w_in, j, _colscale(A_IN, A_Q), lambda t: (t + g_tiles) % a_tiles)
            og, wo, wg, wp = _attn_a(proj, a_sink[j], tail_w, i)
        elif kind == 1:
            proj = _in_proj(h, b_w_in, j, _colscale(B_IN, MIX_WIDTH), lambda t: t)
            og, wo, wg, wp = _attn_b(proj, _rpb_table(b_rpb, j), tail_w, i)
        else:
            lambda_init = 0.8 - 0.6 * math.exp(-0.3 * i)
            proj = _in_proj(h, c_w_in, j, _colscale(C_IN, MIX_WIDTH), lambda t: t)
            og, wo, wg, wp = _attn_c(proj, _alibi_table(), c_lambda, c_subln, j, lambda_init, tail_w, i)
        x2d, h = _tail(og, x2d, p3d, wo, wg, wp, norm_post3, norm_pre3, i)
    return x2d.reshape(BATCH, SEQ, D_MODEL)
```
